```python
import math
import jax, jax.numpy as jnp
from jax import lax
import numpy as np

D_MODEL = 1024
BATCH = 4
SEQ = 8192
DEPTH = 2

N_EVEN = (DEPTH + 1) // 2
N_ODD = DEPTH // 2
HEAD_DIM = 64
CONV_GROUPS = 8
CONV_WIDTH = CONV_GROUPS * HEAD_DIM
ATTN_HEADS = 8
ATTN_WIDTH = ATTN_HEADS * HEAD_DIM
MIX_IN = 3 * CONV_WIDTH + 3 * ATTN_WIDTH
MIX_OUT = CONV_WIDTH + ATTN_WIDTH
SHORT_CONV_K = 3
DILATED_PAIRS = ((128, 1), (512, 4), (2048, 16))
REL_BUCKETS = 32
REL_MAX_DIST = 2048
LRU_WIDTH = D_MODEL
LRU_BLOCKS = 4
LRU_BLOCK = LRU_WIDTH // LRU_BLOCKS
REC_CONV_K = 4
LRU_C = 8.0
D_FF = 2816
PLE_DIM = 256
EPS = 1e-6

kernel_name = "hybrid_conv_dilattn_rglru_macaron"


def rms_norm(x, gain):
    xf = x.astype(jnp.float32)
    y = xf * lax.rsqrt(jnp.mean(xf * xf, axis=-1, keepdims=True) + EPS)
    return (y * gain.astype(jnp.float32)).astype(x.dtype)


def swiglu(h, w_gate, w_up, w_down):
    return (jax.nn.silu(h @ w_gate) * (h @ w_up)) @ w_down


def causal_depthwise_conv(x, w):
    k_taps = w.shape[0]
    s = x.shape[1]
    xp = jnp.pad(x, ((0, 0), (k_taps - 1, 0), (0, 0)))
    y = xp[:, 0:s] * w[0]
    for j in range(1, k_taps):
        y = y + xp[:, j:j + s] * w[j]
    return y


def rel_bucket(dist):
    max_exact = REL_BUCKETS // 2
    n = jnp.maximum(dist, 1).astype(jnp.float32)
    large = max_exact + (jnp.log(n / max_exact) / math.log(REL_MAX_DIST / max_exact)
                         * (REL_BUCKETS - max_exact)).astype(jnp.int32)
    large = jnp.minimum(large, REL_BUCKETS - 1)
    return jnp.where(dist < max_exact, dist, large)


def dilated_branch(q, k, v, rel_bias, window, dilation):
    b_, s_, h_, dh = q.shape
    d = dilation
    nw = window // d
    sub_len = s_ // d
    nb = -(-sub_len // nw)
    lp = nb * nw

    def strided(t, front):
        t = t.reshape(b_, sub_len, d, h_, dh)
        return jnp.pad(t, ((0, 0), (front, lp - sub_len), (0, 0), (0, 0), (0, 0)))

    q_b = strided(q, 0).reshape(b_, nb, nw, d, h_, dh)
    k_p = strided(k, nw).reshape(b_, nb + 1, nw, d, h_, dh)
    v_p = strided(v, nw).reshape(b_, nb + 1, nw, d, h_, dh)
    k_b = jnp.concatenate([k_p[:, :-1], k_p[:, 1:]], axis=2)
    v_b = jnp.concatenate([v_p[:, :-1], v_p[:, 1:]], axis=2)

    scores = jnp.einsum('bnqrhe,bnkrhe->bnrhqk', q_b, k_b).astype(jnp.float32) * (dh ** -0.5)
    qi = jnp.arange(nw)[:, None]
    kj = jnp.arange(2 * nw)[None, :]
    dist = qi + nw - kj
    key_pos = jnp.arange(nb)[:, None, None] * nw + kj[None] - nw
    valid = (dist >= 0)[None] & (dist <= nw)[None] & (key_pos >= 0)
    bucket = rel_bucket(jnp.clip(dist, 0, nw) * d)
    bias = jnp.transpose(rel_bias[bucket].astype(jnp.float32), (2, 0, 1))
    logits = jnp.where(valid[None, :, None, None], scores + bias, -jnp.inf)
    lse = jax.nn.logsumexp(logits, axis=-1)
    probs = jnp.exp(logits - lse[..., None])
    out = jnp.einsum('bnrhqk,bnkrhe->bnqrhe', probs.astype(v.dtype), v_b)
    out = out.reshape(b_, lp, d, h_, dh)[:, :sub_len].reshape(b_, s_, h_, dh)
    lse = jnp.transpose(lse, (0, 1, 4, 2, 3)).reshape(b_, lp, d, h_)[:, :sub_len].reshape(b_, s_, h_)
    return out, lse


def hybrid_mixer(h, w_in, conv_w, q_gain, k_gain, rel_bias, w_out):
    b_, s_, _ = h.shape
    z = h @ w_in
    cuts = np.cumsum([CONV_WIDTH, CONV_WIDTH, CONV_WIDTH, ATTN_WIDTH, ATTN_WIDTH])
    g_b, g_c, c_x, q, k, v = jnp.split(z, cuts, axis=-1)
    y_conv = g_b * causal_depthwise_conv(g_c * c_x, conv_w)
    qh = rms_norm(q.reshape(b_, s_, ATTN_HEADS, HEAD_DIM), q_gain)
    kh = rms_norm(k.reshape(b_, s_, ATTN_HEADS, HEAD_DIM), k_gain)
    vh = v.reshape(b_, s_, ATTN_HEADS, HEAD_DIM)
    outs = []
    lses = []
    for window, dil in DILATED_PAIRS:
        o, l = dilated_branch(qh, kh, vh, rel_bias, window, dil)
        outs.append(o)
        lses.append(l)
    wts = jax.nn.softmax(jnp.stack(lses), axis=0)
    y_attn = jnp.sum(wts[..., None] * jnp.stack(outs).astype(jnp.float32), axis=0)
    y_attn = y_attn.astype(h.dtype).reshape(b_, s_, ATTN_WIDTH)
    return jnp.concatenate([y_conv, y_attn], axis=-1) @ w_out


def rg_lru(xb, wa, ba, wx, bx, lam):
    b_, s_, _ = xb.shape
    xf = xb.astype(jnp.float32)
    xr = xf.reshape(b_, s_, LRU_BLOCKS, LRU_BLOCK)
    gate_a = jnp.einsum('bsgi,gij->bsgj', xr, wa.astype(jnp.float32)).reshape(b_, s_, LRU_WIDTH) + ba.astype(jnp.float32)
    gate_x = jnp.einsum('bsgi,gij->bsgj', xr, wx.astype(jnp.float32)).reshape(b_, s_, LRU_WIDTH) + bx.astype(jnp.float32)
    log_a = -LRU_C * jax.nn.sigmoid(gate_a) * jax.nn.softplus(-lam.astype(jnp.float32))
    a = jnp.exp(log_a)
    u = jnp.sqrt(-jnp.expm1(2.0 * log_a)) * (jax.nn.sigmoid(gate_x) * xf)

    def combine(left, right):
        a1, b1 = left
        a2, b2 = right
        return a1 * a2, a2 * b1 + b2

    _, hs = lax.associative_scan(combine, (a, u), axis=1)
    return hs.astype(xb.dtype)


def recurrent_mixer(h, w_in, conv_w, conv_b, wa, ba, wx, bx, lam, w_out):
    z = h @ w_in
    xb, yb = jnp.split(z, 2, axis=-1)
    xb = causal_depthwise_conv(xb, conv_w) + conv_b
    return (rg_lru(xb, wa, ba, wx, bx, lam) * jax.nn.gelu(yb)) @ w_out


def setup_inputs(seed: int = 0) -> dict:
    key = jax.random.key(seed)
    ks = iter(jax.random.split(key, 40))
    f32 = jnp.float32

    def dense(shape, fan_in):
        return jax.random.normal(next(ks), shape, f32) * (fan_in ** -0.5)

    def gain(shape):
        return 1.0 + 0.02 * jax.random.normal(next(ks), shape, f32)

    def small(shape, scale=0.02):
        return scale * jax.random.normal(next(ks), shape, f32)

    u = jax.random.uniform(next(ks), (N_ODD, LRU_WIDTH), f32, 0.9, 0.999)
    s_base = u ** (1.0 / LRU_C)
    lru_lambda = jnp.log(s_base) - jnp.log1p(-s_base)

    return {
        "x": jax.random.normal(next(ks), (BATCH, SEQ, D_MODEL), f32),
        "p": jax.random.normal(next(ks), (DEPTH, BATCH, SEQ, PLE_DIM), f32),
        "rel_bias": small((REL_BUCKETS, ATTN_HEADS), 0.1),
        "ffn1_norm": gain((DEPTH, D_MODEL)),
        "ffn1_w_gate": dense((DEPTH, D_MODEL, D_FF), D_MODEL),
        "ffn1_w_up": dense((DEPTH, D_MODEL, D_FF), D_MODEL),
        "ffn1_w_down": dense((DEPTH, D_FF, D_MODEL), D_FF),
        "mix_norm": gain((DEPTH, D_MODEL)),
        "hyb_w_in": dense((N_EVEN, D_MODEL, MIX_IN), D_MODEL),
        "hyb_conv_w": dense((N_EVEN, SHORT_CONV_K, CONV_WIDTH), SHORT_CONV_K),
        "hyb_q_gain": gain((N_EVEN, HEAD_DIM)),
        "hyb_k_gain": gain((N_EVEN, HEAD_DIM)),
        "hyb_w_out": dense((N_EVEN, MIX_OUT, D_MODEL), MIX_OUT),
        "rec_w_in": dense((N_ODD, D_MODEL, 2 * LRU_WIDTH), D_MODEL),
        "rec_conv_w": dense((N_ODD, REC_CONV_K, LRU_WIDTH), REC_CONV_K),
        "rec_conv_b": small((N_ODD, LRU_WIDTH)),
        "lru_wa": dense((N_ODD, LRU_BLOCKS, LRU_BLOCK, LRU_BLOCK), LRU_BLOCK),
        "lru_ba": small((N_ODD, LRU_WIDTH)),
        "lru_wx": dense((N_ODD, LRU_BLOCKS, LRU_BLOCK, LRU_BLOCK), LRU_BLOCK),
        "lru_bx": small((N_ODD, LRU_WIDTH)),
        "lru_lambda": lru_lambda,
        "rec_w_out": dense((N_ODD, LRU_WIDTH, D_MODEL), LRU_WIDTH),
        "ffn2_norm": gain((DEPTH, D_MODEL)),
        "ffn2_w_gate": dense((DEPTH, D_MODEL, D_FF), D_MODEL),
        "ffn2_w_up": dense((DEPTH, D_MODEL, D_FF), D_MODEL),
        "ffn2_w_down": dense((DEPTH, D_FF, D_MODEL), D_FF),
        "ple_norm": gain((DEPTH, D_MODEL)),
        "ple_w_gate": dense((DEPTH, D_MODEL, D_MODEL), D_MODEL),
        "ple_w_proj": dense((DEPTH, PLE_DIM, D_MODEL), PLE_DIM),
    }


def reference(x, p, rel_bias, ffn1_norm, ffn1_w_gate, ffn1_w_up, ffn1_w_down, mix_norm,
              hyb_w_in, hyb_conv_w, hyb_q_gain, hyb_k_gain, hyb_w_out,
              rec_w_in, rec_conv_w, rec_conv_b, lru_wa, lru_ba, lru_wx, lru_bx, lru_lambda, rec_w_out,
              ffn2_norm, ffn2_w_gate, ffn2_w_up, ffn2_w_down, ple_norm, ple_w_gate, ple_w_proj):
    h = x
    for i in range(DEPTH):
        h = h + 0.5 * swiglu(rms_norm(h, ffn1_norm[i]), ffn1_w_gate[i], ffn1_w_up[i], ffn1_w_down[i])
        hn = rms_norm(h, mix_norm[i])
        if i % 2 == 0:
            e = i // 2
            h = h + hybrid_mixer(hn, hyb_w_in[e], hyb_conv_w[e], hyb_q_gain[e], hyb_k_gain[e],
                                 rel_bias, hyb_w_out[e])
        else:
            o = i // 2
            h = h + recurrent_mixer(hn, rec_w_in[o], rec_conv_w[o], rec_conv_b[o], lru_wa[o], lru_ba[o],
                                    lru_wx[o], lru_bx[o], lru_lambda[o], rec_w_out[o])
        h = h + 0.5 * swiglu(rms_norm(h, ffn2_norm[i]), ffn2_w_gate[i], ffn2_w_up[i], ffn2_w_down[i])
        gate = jax.nn.sigmoid(rms_norm(h, ple_norm[i]) @ ple_w_gate[i])
        h = h + gate * (p[i] @ ple_w_proj[i])
    return h
```

```python
import functools
import math

import jax
import jax.numpy as jnp
from jax import lax
from jax.experimental import pallas as pl
from jax.experimental.pallas import tpu as pltpu

F32 = jnp.float32
BF16 = jnp.bfloat16

EPS = 1e-6
HEAD_DIM = 64
N_HEADS = 8
ATTN_WIDTH = N_HEADS * HEAD_DIM
CONV_WIDTH = 512
DILATIONS = (1, 4, 16)
KEYS_BACK = 128
REL_BUCKETS = 32
REL_MAX_DIST = 2048
LRU_BLOCKS = 4
LRU_C = 8.0
LANES = 128
N_SLABS = ATTN_WIDTH // LANES
ATTN_CHUNK = KEYS_BACK * DILATIONS[-1]
MASKED = -1e30
VMEM_LIMIT = 56 * 1024 * 1024

TOKEN_TILE = 512


def _rms(x, gain):
    ms = jnp.mean(x * x, axis=-1, keepdims=True)
    return x * lax.rsqrt(ms + EPS) * gain


def _dot(a, b):
    return jnp.dot(a, b, preferred_element_type=F32)


def _resident(shape):
    nd = len(shape)
    return pl.BlockSpec(shape, lambda *_: (0,) * nd, pipeline_mode=pl.Buffered(1))


def _ffn_kernel(has_mix, has_ple, *refs):
    it = iter(refs)
    x_ref = next(it)
    if has_mix:
        yc_ref, ya_ref, wo_ref = next(it), next(it), next(it)
    g_ref, wg_ref, wu_ref, wd_ref = next(it), next(it), next(it), next(it)
    if has_ple:
        p_ref, pg_ref, pwg_ref, pwp_ref = next(it), next(it), next(it), next(it)
    o_ref = next(it)

    x = x_ref[...]
    if has_mix:
        ya = jnp.concatenate([ya_ref[j] for j in range(N_SLABS)], axis=-1).astype(BF16)
        ymix = jnp.concatenate([yc_ref[...], ya], axis=-1)
        x = x + _dot(ymix, wo_ref[...])
    hn = _rms(x, g_ref[...]).astype(BF16)
    gate = _dot(hn, wg_ref[...])
    up = _dot(hn, wu_ref[...])
    act = (gate * jax.nn.sigmoid(gate) * up).astype(BF16)
    x = x + 0.5 * _dot(act, wd_ref[...])
    if has_ple:
        hn2 = _rms(x, pg_ref[...]).astype(BF16)
        pgate = jax.nn.sigmoid(_dot(hn2, pwg_ref[...]))
        x = x + pgate * _dot(p_ref[...].astype(BF16), pwp_ref[...])
    o_ref[...] = x


def _ffn(h, norm, w_gate, w_up, w_down, mix=None, ple=None, *, seq_tiles):
    t, d = h.shape
    tm = TOKEN_TILE
    tok = lambda i: (i, 0)
    args, specs = [h], [pl.BlockSpec((tm, d), tok)]
    if mix is not None:
        yc, ya, w_out = mix
        args += [yc, ya, w_out]
        specs += [pl.BlockSpec((tm, yc.shape[1]), tok),
                  pl.BlockSpec((None, N_SLABS, tm, LANES), lambda i: (i // seq_tiles, 0, i % seq_tiles, 0)),
                  _resident(w_out.shape)]
    args += [norm, w_gate, w_up, w_down]
    specs += [_resident(norm.shape), _resident(w_gate.shape), _resident(w_up.shape), _resident(w_down.shape)]
    if ple is not None:
        p, pnorm, pw_gate, pw_proj = ple
        args += [p, pnorm, pw_gate, pw_proj]
        specs += [pl.BlockSpec((tm, p.shape[1]), tok), _resident(pnorm.shape),
                  _resident(pw_gate.shape), _resident(pw_proj.shape)]
    return pl.pallas_call(
        functools.partial(_ffn_kernel, mix is not None, ple is not None),
        grid=(t // tm,),
        in_specs=specs,
        out_specs=pl.BlockSpec((tm, d), tok),
        out_shape=jax.ShapeDtypeStruct((t, d), F32),
        compiler_params=pltpu.CompilerParams(dimension_semantics=("parallel",), vmem_limit_bytes=VMEM_LIMIT),
        name="ffn" + ("_mix" if mix is not None else "") + ("_ple" if ple is not None else ""),
    )(*args)


def _shift_rows(u, prev, k):
    rolled = pltpu.roll(u, shift=k, axis=0)
    row = lax.broadcasted_iota(jnp.int32, u.shape, 0)
    for j in range(k):
        rolled = jnp.where(row == j, prev[8 - k + j:8 - k + j + 1, :], rolled)
    return rolled


def _head_mean_sq(x, blockdiag):
    x2 = x * x
    hi = x2.astype(BF16)
    lo = (x2 - hi.astype(F32)).astype(BF16)
    return _dot(hi, blockdiag) + _dot(lo, blockdiag)


def _mix_in_kernel(h_ref, g_ref, win_ref, cw_ref, qg_ref, kg_ref, bd_ref,
                   yc_ref, q_ref, k_ref, v_ref, carry_ref):
    @pl.when(pl.program_id(1) == 0)
    def _():
        carry_ref[...] = jnp.zeros_like(carry_ref)

    cwid = CONV_WIDTH
    hn = _rms(h_ref[...], g_ref[...]).astype(BF16)
    z = _dot(hn, win_ref[...])
    g_b, g_c, c_x = z[:, 0:cwid], z[:, cwid:2 * cwid], z[:, 2 * cwid:3 * cwid]
    u = g_c * c_x
    prev = carry_ref[...]
    cw = cw_ref[...]
    conv = cw[0:1] * _shift_rows(u, prev, 2) + cw[1:2] * _shift_rows(u, prev, 1) + cw[2:3] * u
    yc_ref[...] = (g_b * conv).astype(BF16)
    carry_ref[...] = u[u.shape[0] - 8:, :]

    q = z[:, 3 * cwid:3 * cwid + ATTN_WIDTH]
    k = z[:, 3 * cwid + ATTN_WIDTH:3 * cwid + 2 * ATTN_WIDTH]
    v = z[:, 3 * cwid + 2 * ATTN_WIDTH:]
    bd = bd_ref[...]
    qn = q * lax.rsqrt(_head_mean_sq(q, bd) + EPS) * (qg_ref[...] * HEAD_DIM ** -0.5)
    kn = k * lax.rsqrt(_head_mean_sq(k, bd) + EPS) * kg_ref[...]
    for j in range(N_SLABS):
        q_ref[j] = qn[:, j * LANES:(j + 1) * LANES]
        k_ref[j] = kn[:, j * LANES:(j + 1) * LANES]
        v_ref[j] = v[:, j * LANES:(j + 1) * LANES]


def _mix_in(h, norm, w_in, conv_w, q_gain, k_gain, *, batch, seq):
    t, d = h.shape
    tm = TOKEN_TILE
    nts = seq // tm
    tok = lambda b, j: (b * nts + j, 0)
    slab = pl.BlockSpec((None, N_SLABS, tm, LANES), lambda b, j: (b, 0, j, 0))
    head_of = jnp.arange(ATTN_WIDTH) // HEAD_DIM
    blockdiag = jnp.where(head_of[:, None] == head_of[None, :], 1.0 / HEAD_DIM, 0.0).astype(BF16)
    qg = jnp.tile(q_gain, N_HEADS)[None, :]
    kg = jnp.tile(k_gain, N_HEADS)[None, :]
    slab_shape = jax.ShapeDtypeStruct((batch, N_SLABS, seq, LANES), F32)
    return pl.pallas_call(
        _mix_in_kernel,
        grid=(batch, nts),
        in_specs=[pl.BlockSpec((tm, d), tok), _resident(norm.shape), _resident(w_in.shape),
                  _resident(conv_w.shape), _resident(qg.shape), _resident(kg.shape), _resident(blockdiag.shape)],
        out_specs=[pl.BlockSpec((tm, CONV_WIDTH), tok), slab, slab, slab],
        out_shape=[jax.ShapeDtypeStruct((t, CONV_WIDTH), BF16), slab_shape, slab_shape, slab_shape],
        scratch_shapes=[pltpu.VMEM((8, CONV_WIDTH), F32)],
        compiler_params=pltpu.CompilerParams(dimension_semantics=("parallel", "arbitrary"),
                                             vmem_limit_bytes=VMEM_LIMIT),
        name="mix_in",
    )(h, norm, w_in, conv_w, qg, kg, blockdiag)


def _rel_bucket(dist):
    max_exact = REL_BUCKETS // 2
    n = jnp.maximum(dist, 1).astype(F32)
    large = max_exact + (jnp.log(n / max_exact) / math.log(REL_MAX_DIST / max_exact)
                         * (REL_BUCKETS - max_exact)).astype(jnp.int32)
    large = jnp.minimum(large, REL_BUCKETS - 1)
    return jnp.where(dist < max_exact, dist, large)


def _bias_table(rel_bias):
    qi = jnp.arange(KEYS_BACK)[:, None]
    kj = jnp.arange(2 * KEYS_BACK)[None, :]
    dist = qi + KEYS_BACK - kj
    valid = (dist >= 0) & (dist <= KEYS_BACK)
    tabs = []
    for d in DILATIONS:
        bucket = _rel_bucket(jnp.clip(dist, 0, KEYS_BACK) * d)
        bias = jnp.transpose(rel_bias[bucket].astype(F32), (2, 0, 1))
        tabs.append(jnp.stack([jnp.where(valid[None], bias, MASKED),
                               jnp.where((valid & (kj >= KEYS_BACK))[None], bias, MASKED)]))
    return jnp.stack(tabs)


def _attn_kernel(q_ref, kp_ref, kc_ref, vp_ref, vc_ref, bias_ref, o_ref,
                 k_scr, v_scr, acc_scr, m_scr, l_scr):
    ch = ATTN_CHUNK
    k_scr[0:ch, :] = kp_ref[...]
    k_scr[ch:2 * ch, :] = kc_ref[...]
    v_scr[0:ch, :] = vp_ref[...]
    v_scr[ch:2 * ch, :] = vc_ref[...]
    first_chunk = pl.program_id(2) == 0
    head_a = lax.broadcasted_iota(jnp.int32, (1, LANES), 1) < HEAD_DIM
    ones = jnp.ones((2 * KEYS_BACK, LANES), BF16)
    n_units = ch // KEYS_BACK

    for g, d in enumerate(DILATIONS):
        blocks = ch // (KEYS_BACK * d)

        def rows(start, size, d=d):
            return pl.ds(start, size) if d == 1 else pl.ds(start, size, stride=d)

        def unit(t, carry, g=g, d=d, blocks=blocks, rows=rows):
            r, i = t // blocks, t % blocks
            q_rows = rows(r + KEYS_BACK * d * i, KEYS_BACK)
            kv_rows = rows(ch + r + KEYS_BACK * d * (i - 1), 2 * KEYS_BACK)
            q = q_ref[q_rows, :].astype(BF16)
            k = k_scr[kv_rows, :]
            v = v_scr[kv_rows, :].astype(BF16)
            v_aug = jnp.concatenate([v, ones], axis=-1)
            variant = jnp.where(jnp.logical_and(first_chunk, i == 0), 1, 0)
            halves = []
            for hh in range(2):
                k_h = jnp.where(head_a if hh == 0 else ~head_a, k, 0.0).astype(BF16)
                s = lax.dot_general(q, k_h, (((1,), (1,)), ((), ())), preferred_element_type=F32)
                s = s + bias_ref[g, variant, hh]
                m = jnp.max(s, axis=-1, keepdims=True)
                p = jnp.exp(s - m).astype(BF16)
                pv = _dot(p, v_aug)
                halves.append((pv[:, :LANES], pv[:, LANES:], m))
            acc = jnp.where(head_a, halves[0][0], halves[1][0])
            l = jnp.where(head_a, halves[0][1], halves[1][1])
            m = jnp.where(head_a, halves[0][2], halves[1][2])
            if g > 0:
                m_old = m_scr[q_rows, :]
                m_new = jnp.maximum(m_old, m)
                a_old, a_new = jnp.exp(m_old - m_new), jnp.exp(m - m_new)
                acc = acc_scr[q_rows, :] * a_old + acc * a_new
                l = l_scr[q_rows, :] * a_old + l * a_new
                m = m_new
            if g < len(DILATIONS) - 1:
                acc_scr[q_rows, :] = acc
                l_scr[q_rows, :] = l
                m_scr[q_rows, :] = m
            else:
                o_ref[q_rows, :] = acc / l
            return carry

        lax.fori_loop(0, n_units, unit, 0)


def _attention(q, k, v, bias_tab):
    batch, _, seq, _ = q.shape
    ch = ATTN_CHUNK
    nch = seq // ch
    cur = pl.BlockSpec((None, None, ch, LANES), lambda b, s, c: (b, s, c, 0))
    prv = pl.BlockSpec((None, None, ch, LANES), lambda b, s, c: (b, s, jnp.maximum(c - 1, 0), 0))
    bias_spec = pl.BlockSpec((len(DILATIONS), 2, 2, KEYS_BACK, 2 * KEYS_BACK), lambda b, s, c: (0, 0, s, 0, 0))
    return pl.pallas_call(
        _attn_kernel,
        grid=(batch, N_SLABS, nch),
        in_specs=[cur, prv, cur, prv, cur, bias_spec],
        out_specs=cur,
        out_shape=jax.ShapeDtypeStruct(q.shape, F32),
        scratch_shapes=[pltpu.VMEM((2 * ch, LANES), F32), pltpu.VMEM((2 * ch, LANES), F32),
                        pltpu.VMEM((ch, LANES), F32), pltpu.VMEM((ch, LANES), F32), pltpu.VMEM((ch, LANES), F32)],
        compiler_params=pltpu.CompilerParams(dimension_semantics=("parallel", "parallel", "parallel"),
                                             vmem_limit_bytes=VMEM_LIMIT),
        name="dilated_attention",
    )(q, k, k, v, v, bias_tab)


def _rec_kernel(h_ref, g_ref, win_ref, cw_ref, cb_ref, wa_ref, ba_ref, wx_ref, bx_ref, lam_ref, wout_ref,
                o_ref, xcarry_ref, hcarry_ref):
    @pl.when(pl.program_id(1) == 0)
    def _():
        xcarry_ref[...] = jnp.zeros_like(xcarry_ref)
        hcarry_ref[...] = jnp.zeros_like(hcarry_ref)

    h = h_ref[...]
    tm, width = h.shape
    hn = _rms(h, g_ref[...]).astype(BF16)
    z = _dot(hn, win_ref[...])
    x_pre, yb = z[:, :width], z[:, width:]
    prev = xcarry_ref[...]
    cw = cw_ref[...]
    xb = (cw[0:1] * _shift_rows(x_pre, prev, 3) + cw[1:2] * _shift_rows(x_pre, prev, 2)
          + cw[2:3] * _shift_rows(x_pre, prev, 1) + cw[3:4] * x_pre) + cb_ref[...]
    xcarry_ref[...] = x_pre[tm - 8:, :]

    blk = width // LRU_BLOCKS
    xb16 = xb.astype(BF16)
    gate_a = jnp.concatenate([_dot(xb16[:, n * blk:(n + 1) * blk], wa_ref[n]) for n in range(LRU_BLOCKS)],
                             axis=-1) + ba_ref[...]
    gate_x = jnp.concatenate([_dot(xb16[:, n * blk:(n + 1) * blk], wx_ref[n]) for n in range(LRU_BLOCKS)],
                             axis=-1) + bx_ref[...]
    neg_lam = -lam_ref[...]
    softplus = jnp.maximum(neg_lam, 0.0) + jnp.log1p(jnp.exp(-jnp.abs(neg_lam)))
    log_a = -LRU_C * jax.nn.sigmoid(gate_a) * softplus
    a = jnp.exp(log_a)
    b = jnp.sqrt(jnp.tanh(-log_a) * (1.0 + a * a)) * (jax.nn.sigmoid(gate_x) * xb)

    row = lax.broadcasted_iota(jnp.int32, (tm, width), 0)
    shift = 1
    while shift < tm:
        a_prev = pltpu.roll(a, shift=shift, axis=0)
        b_prev = pltpu.roll(b, shift=shift, axis=0)
        keep = row >= shift
        b = jnp.where(keep, a * b_prev + b, b)
        a = jnp.where(keep, a * a_prev, a)
        shift *= 2
    hs = a * hcarry_ref[7:8, :] + b
    hcarry_ref[...] = hs[tm - 8:, :]

    y = (hs * jax.nn.gelu(yb)).astype(BF16)
    o_ref[...] = h + _dot(y, wout_ref[...])


def _recurrent(h, norm, w_in, conv_w, conv_b, wa, ba, wx, bx, lam, w_out, *, batch, seq):
    t, d = h.shape
    tm = TOKEN_TILE
    nts = seq // tm
    tok = pl.BlockSpec((tm, d), lambda b, j: (b * nts + j, 0))
    consts = [norm, w_in, conv_w, conv_b, wa, ba, wx, bx, lam, w_out]
    return pl.pallas_call(
        _rec_kernel,
        grid=(batch, nts),
        in_specs=[tok] + [_resident(c.shape) for c in consts],
        out_specs=tok,
        out_shape=jax.ShapeDtypeStruct((t, d), F32),
        scratch_shapes=[pltpu.VMEM((8, d), F32), pltpu.VMEM((8, d), F32)],
        compiler_params=pltpu.CompilerParams(dimension_semantics=("parallel", "arbitrary"),
                                             vmem_limit_bytes=VMEM_LIMIT),
        name="recurrent",
    )(h, *consts)


def kernel(x, p, rel_bias, ffn1_norm, ffn1_w_gate, ffn1_w_up, ffn1_w_down, mix_norm, hyb_w_in, hyb_conv_w, hyb_q_gain, hyb_k_gain, hyb_w_out, rec_w_in, rec_conv_w, rec_conv_b, lru_wa, lru_ba, lru_wx, lru_bx, lru_lambda, rec_w_out, ffn2_norm, ffn2_w_gate, ffn2_w_up, ffn2_w_down, ple_norm, ple_w_gate, ple_w_proj):
    batch, seq, d = x.shape
    depth = p.shape[0]
    t = batch * seq
    assert seq % ATTN_CHUNK == 0 and seq % TOKEN_TILE == 0
    seq_tiles = seq // TOKEN_TILE
    row = lambda a: a[None, :].astype(F32)
    w16 = lambda a: a.astype(BF16)

    bias_tab = _bias_table(rel_bias)
    h = x.reshape(t, d)
    for i in range(depth):
        h = _ffn(h, row(ffn1_norm[i]), w16(ffn1_w_gate[i]), w16(ffn1_w_up[i]), w16(ffn1_w_down[i]),
                 seq_tiles=seq_tiles)
        ple = (p[i].reshape(t, -1), row(ple_norm[i]), w16(ple_w_gate[i]), w16(ple_w_proj[i]))
        ffn2 = (row(ffn2_norm[i]), w16(ffn2_w_gate[i]), w16(ffn2_w_up[i]), w16(ffn2_w_down[i]))
        if i % 2 == 0:
            e = i // 2
            yc, q, k, v = _mix_in(h, row(mix_norm[i]), w16(hyb_w_in[e]), hyb_conv_w[e],
                                  hyb_q_gain[e], hyb_k_gain[e], batch=batch, seq=seq)
            ya = _attention(q, k, v, bias_tab)
            h = _ffn(h, *ffn2, mix=(yc, ya, w16(hyb_w_out[e])), ple=ple, seq_tiles=seq_tiles)
        else:
            o = i // 2
            h = _recurrent(h, row(mix_norm[i]), w16(rec_w_in[o]), rec_conv_w[o], row(rec_conv_b[o]),
                           w16(lru_wa[o]), row(lru_ba[o]), w16(lru_wx[o]), row(lru_bx[o]),
                           row(lru_lambda[o]), w16(rec_w_out[o]), batch=batch, seq=seq)
            h = _ffn(h, *ffn2, ple=ple, seq_tiles=seq_tiles)
    return h.reshape(batch, seq, d)
```

```python
import functools
import math

import jax
import jax.numpy as jnp
from jax import lax
from jax.experimental import pallas as pl
from jax.experimental.pallas import tpu as pltpu

F32 = jnp.float32
BF16 = jnp.bfloat16

EPS = 1e-6
HEAD_DIM = 64
N_HEADS = 8
ATTN_WIDTH = N_HEADS * HEAD_DIM
CONV_WIDTH = 512
DILATIONS = (1, 4, 16)
KEYS_BACK = 128
REL_BUCKETS = 32
REL_MAX_DIST = 2048
LRU_BLOCKS = 4
LRU_C = 8.0
LOG2_E = math.log2(math.e)
LANES = 128
SUBLANES = 8
N_SLABS = ATTN_WIDTH // LANES
ATTN_CHUNK = KEYS_BACK * DILATIONS[-1]
MASKED = -1e30
VMEM_LIMIT = 56 * 1024 * 1024

TOKEN_TILE = 512


def _rms(x, gain):
    ms = jnp.mean(x * x, axis=-1, keepdims=True)
    return x * lax.rsqrt(ms + EPS) * gain


def _dot(a, b):
    return jnp.dot(a, b, preferred_element_type=F32)


def _resident(shape):
    nd = len(shape)
    return pl.BlockSpec(shape, lambda *_: (0,) * nd, pipeline_mode=pl.Buffered(1))


def _ffn_kernel(has_mix, has_ple, *refs):
    it = iter(refs)
    x_ref = next(it)
    if has_mix:
        yc_ref, ya_ref, wo_ref = next(it), next(it), next(it)
    g_ref, wg_ref, wu_ref, wd_ref = next(it), next(it), next(it), next(it)
    if has_ple:
        p_ref, pg_ref, pwg_ref, pwp_ref = next(it), next(it), next(it), next(it)
    o_ref = next(it)

    x = x_ref[...]
    if has_mix:
        ya = jnp.concatenate([ya_ref[j] for j in range(N_SLABS)], axis=-1).astype(BF16)
        ymix = jnp.concatenate([yc_ref[...], ya], axis=-1)
        x = x + _dot(ymix, wo_ref[...])
    hn = _rms(x, g_ref[...]).astype(BF16)
    gate = _dot(hn, wg_ref[...])
    up = _dot(hn, wu_ref[...])
    act = (gate * jax.nn.sigmoid(gate) * up).astype(BF16)
    x = x + 0.5 * _dot(act, wd_ref[...])
    if has_ple:
        hn2 = _rms(x, pg_ref[...]).astype(BF16)
        pgate = jax.nn.sigmoid(_dot(hn2, pwg_ref[...]))
        x = x + pgate * _dot(p_ref[...].astype(BF16), pwp_ref[...])
    o_ref[...] = x


def _ffn(h, norm, w_gate, w_up, w_down, mix=None, ple=None, *, seq_tiles):
    t, d = h.shape
    tm = TOKEN_TILE
    tok = lambda i: (i, 0)
    args, specs = [h], [pl.BlockSpec((tm, d), tok)]
    if mix is not None:
        yc, ya, w_out = mix
        args += [yc, ya, w_out]
        specs += [pl.BlockSpec((tm, yc.shape[1]), tok),
                  pl.BlockSpec((None, N_SLABS, tm, LANES), lambda i: (i // seq_tiles, 0, i % seq_tiles, 0)),
                  _resident(w_out.shape)]
    args += [norm, w_gate, w_up, w_down]
    specs += [_resident(norm.shape), _resident(w_gate.shape), _resident(w_up.shape), _resident(w_down.shape)]
    if ple is not None:
        p, pnorm, pw_gate, pw_proj = ple
        args += [p, pnorm, pw_gate, pw_proj]
        specs += [pl.BlockSpec((tm, p.shape[1]), tok), _resident(pnorm.shape),
                  _resident(pw_gate.shape), _resident(pw_proj.shape)]
    return pl.pallas_call(
        functools.partial(_ffn_kernel, mix is not None, ple is not None),
        grid=(t // tm,),
        in_specs=specs,
        out_specs=pl.BlockSpec((tm, d), tok),
        out_shape=jax.ShapeDtypeStruct((t, d), F32),
        compiler_params=pltpu.CompilerParams(dimension_semantics=("parallel",), vmem_limit_bytes=VMEM_LIMIT),
        name="ffn" + ("_mix" if mix is not None else "") + ("_ple" if ple is not None else ""),
    )(*args)


def _shift_rows(u, prev, k):
    rolled = pltpu.roll(u, shift=k, axis=0)
    row = lax.broadcasted_iota(jnp.int32, u.shape, 0)
    for j in range(k):
        rolled = jnp.where(row == j, prev[8 - k + j:8 - k + j + 1, :], rolled)
    return rolled


def _head_mean_sq(x, blockdiag):
    x2 = x * x
    hi = x2.astype(BF16)
    lo = (x2 - hi.astype(F32)).astype(BF16)
    return _dot(hi, blockdiag) + _dot(lo, blockdiag)


def _mix_in_kernel(h_ref, g_ref, win_ref, cw_ref, qg_ref, kg_ref, bd_ref,
                   yc_ref, q_ref, k_ref, v_ref, carry_ref):
    @pl.when(pl.program_id(1) == 0)
    def _():
        carry_ref[...] = jnp.zeros_like(carry_ref)

    cwid = CONV_WIDTH
    hn = _rms(h_ref[...], g_ref[...]).astype(BF16)
    z = _dot(hn, win_ref[...])
    g_b, g_c, c_x = z[:, 0:cwid], z[:, cwid:2 * cwid], z[:, 2 * cwid:3 * cwid]
    u = g_c * c_x
    prev = carry_ref[...]
    cw = cw_ref[...]
    conv = cw[0:1] * _shift_rows(u, prev, 2) + cw[1:2] * _shift_rows(u, prev, 1) + cw[2:3] * u
    yc_ref[...] = (g_b * conv).astype(BF16)
    carry_ref[...] = u[u.shape[0] - 8:, :]

    q = z[:, 3 * cwid:3 * cwid + ATTN_WIDTH]
    k = z[:, 3 * cwid + ATTN_WIDTH:3 * cwid + 2 * ATTN_WIDTH]
    v = z[:, 3 * cwid + 2 * ATTN_WIDTH:]
    bd = bd_ref[...]
    qn = q * lax.rsqrt(_head_mean_sq(q, bd) + EPS) * (qg_ref[...] * (HEAD_DIM ** -0.5 * LOG2_E))
    kn = k * lax.rsqrt(_head_mean_sq(k, bd) + EPS) * kg_ref[...]
    for j in range(N_SLABS):
        q_ref[j] = qn[:, j * LANES:(j + 1) * LANES]
        k_ref[j] = kn[:, j * LANES:(j + 1) * LANES]
        v_ref[j] = v[:, j * LANES:(j + 1) * LANES]


def _mix_in(h, norm, w_in, conv_w, q_gain, k_gain, *, batch, seq):
    t, d = h.shape
    tm = TOKEN_TILE
    nts = seq // tm
    tok = lambda b, j: (b * nts + j, 0)
    slab = pl.BlockSpec((None, N_SLABS, tm, LANES), lambda b, j: (b, 0, j, 0))
    head_of = jnp.arange(ATTN_WIDTH) // HEAD_DIM
    blockdiag = jnp.where(head_of[:, None] == head_of[None, :], 1.0 / HEAD_DIM, 0.0).astype(BF16)
    qg = jnp.tile(q_gain, N_HEADS)[None, :]
    kg = jnp.tile(k_gain, N_HEADS)[None, :]
    slab_shape = jax.ShapeDtypeStruct((batch, N_SLABS, seq, LANES), F32)
    return pl.pallas_call(
        _mix_in_kernel,
        grid=(batch, nts),
        in_specs=[pl.BlockSpec((tm, d), tok), _resident(norm.shape), _resident(w_in.shape),
                  _resident(conv_w.shape), _resident(qg.shape), _resident(kg.shape), _resident(blockdiag.shape)],
        out_specs=[pl.BlockSpec((tm, CONV_WIDTH), tok), slab, slab, slab],
        out_shape=[jax.ShapeDtypeStruct((t, CONV_WIDTH), BF16), slab_shape, slab_shape, slab_shape],
        scratch_shapes=[pltpu.VMEM((8, CONV_WIDTH), F32)],
        compiler_params=pltpu.CompilerParams(dimension_semantics=("parallel", "arbitrary"),
                                             vmem_limit_bytes=VMEM_LIMIT),
        name="mix_in",
    )(h, norm, w_in, conv_w, qg, kg, blockdiag)


def _rel_bucket(dist):
    max_exact = REL_BUCKETS // 2
    n = jnp.maximum(dist, 1).astype(F32)
    large = max_exact + (jnp.log(n / max_exact) / math.log(REL_MAX_DIST / max_exact)
                         * (REL_BUCKETS - max_exact)).astype(jnp.int32)
    large = jnp.minimum(large, REL_BUCKETS - 1)
    return jnp.where(dist < max_exact, dist, large)


def _bias_table(rel_bias):
    qi = jnp.arange(KEYS_BACK)[:, None]
    kj = jnp.arange(2 * KEYS_BACK)[None, :]
    dist = qi + KEYS_BACK - kj
    valid = (dist >= 0) & (dist <= KEYS_BACK)
    tabs = []
    for d in DILATIONS:
        bucket = _rel_bucket(jnp.clip(dist, 0, KEYS_BACK) * d)
        onehot = (bucket[..., None] == jnp.arange(REL_BUCKETS)).astype(F32)
        bias = jnp.einsum("qkb,bh->hqk", onehot, rel_bias.astype(F32) * LOG2_E,
                          precision=lax.Precision.HIGHEST)
        tabs.append(jnp.stack([jnp.where(valid[None], bias, MASKED),
                               jnp.where((valid & (kj >= KEYS_BACK))[None], bias, MASKED)]))
    tab = jnp.stack(tabs)
    return tab.reshape(len(DILATIONS), 2, N_SLABS, 2 * KEYS_BACK, 2 * KEYS_BACK)


def _attn_kernel(q_ref, kp_ref, kc_ref, vp_ref, vc_ref, bias_ref, o_ref,
                 k_scr, v_scr, acc_scr, m_scr, l_scr):
    ch = ATTN_CHUNK
    k_scr[0:ch, :] = kp_ref[...]
    k_scr[ch:2 * ch, :] = kc_ref[...]
    v_scr[0:ch, :] = vp_ref[...]
    v_scr[ch:2 * ch, :] = vc_ref[...]
    first_chunk = pl.program_id(2) == 0
    head_a = lax.broadcasted_iota(jnp.int32, (1, LANES), 1) < HEAD_DIM
    ones = jnp.ones((2 * KEYS_BACK, LANES), BF16)
    n_units = ch // KEYS_BACK

    def unit(g, d, t):
        blocks = ch // (KEYS_BACK * d)
        rows = (lambda s, n: pl.ds(s, n)) if d == 1 else (lambda s, n: pl.ds(s, n, stride=d))
        r, i = t // blocks, t % blocks
        q_rows = rows(r + KEYS_BACK * d * i, KEYS_BACK)
        kv_rows = rows(ch + r + KEYS_BACK * d * (i - 1), 2 * KEYS_BACK)
        q = q_ref[q_rows, :]
        q2 = jnp.concatenate([jnp.where(head_a, q, 0.0), jnp.where(head_a, 0.0, q)], axis=0).astype(BF16)
        k = k_scr[kv_rows, :].astype(BF16)
        v = v_scr[kv_rows, :].astype(BF16)
        v_aug = jnp.concatenate([v, ones], axis=-1)
        variant = jnp.where(jnp.logical_and(first_chunk, i == 0), 1, 0)
        s = lax.dot_general(q2, k, (((1,), (1,)), ((), ())), preferred_element_type=F32)
        s = s + bias_ref[g, variant]
        m = jnp.max(s, axis=-1, keepdims=True)
        p = jnp.exp2(s - m).astype(BF16)
        pv = _dot(p, v_aug)
        nq = KEYS_BACK
        acc_scr[g, q_rows, :] = jnp.where(head_a, pv[:nq, :LANES], pv[nq:, :LANES])
        l_scr[g, q_rows, :] = jnp.where(head_a, pv[:nq, LANES:], pv[nq:, LANES:])
        m_scr[g, q_rows, :] = jnp.where(head_a, m[:nq], m[nq:])

    def all_branches(t, carry):
        for g, d in enumerate(DILATIONS):
            unit(g, d, t)
        return carry

    lax.fori_loop(0, n_units, all_branches, 0, unroll=4)

    def merge(t, carry):
        rows = pl.ds(pl.multiple_of(t * KEYS_BACK, KEYS_BACK), KEYS_BACK)
        ms = [m_scr[g, rows, :] for g in range(len(DILATIONS))]
        m_all = functools.reduce(jnp.maximum, ms)
        ws = [jnp.exp2(m - m_all) for m in ms]
        num = sum(w * acc_scr[g, rows, :] for g, w in enumerate(ws))
        den = sum(w * l_scr[g, rows, :] for g, w in enumerate(ws))
        o_ref[rows, :] = num / den
        return carry

    lax.fori_loop(0, n_units, merge, 0)


def _attention(q, k, v, bias_tab):
    batch, _, seq, _ = q.shape
    ch = ATTN_CHUNK
    nch = seq // ch
    nb = len(DILATIONS)
    cur = pl.BlockSpec((None, None, ch, LANES), lambda b, s, c: (b, s, c, 0))
    prv = pl.BlockSpec((None, None, ch, LANES), lambda b, s, c: (b, s, jnp.maximum(c - 1, 0), 0))
    bias_spec = pl.BlockSpec((nb, 2, None, 2 * KEYS_BACK, 2 * KEYS_BACK), lambda b, s, c: (0, 0, s, 0, 0))
    return pl.pallas_call(
        _attn_kernel,
        grid=(batch, N_SLABS, nch),
        in_specs=[cur, prv, cur, prv, cur, bias_spec],
        out_specs=cur,
        out_shape=jax.ShapeDtypeStruct(q.shape, F32),
        scratch_shapes=[pltpu.VMEM((2 * ch, LANES), F32), pltpu.VMEM((2 * ch, LANES), F32),
                        pltpu.VMEM((nb, ch, LANES), F32), pltpu.VMEM((nb, ch, LANES), F32),
                        pltpu.VMEM((nb, ch, LANES), F32)],
        compiler_params=pltpu.CompilerParams(dimension_semantics=("parallel", "parallel", "parallel"),
                                             vmem_limit_bytes=VMEM_LIMIT),
        name="dilated_attention",
    )(q, k, k, v, v, bias_tab)


def _rec_kernel(h_ref, g_ref, win_ref, cw_ref, cb_ref, wa_ref, ba_ref, wx_ref, bx_ref, lam_ref, wout_ref,
                o_ref, x_scr, a_scr, b_scr, hl_scr, pl_scr, carry_scr):
    tm, width = h_ref.shape
    n_slabs = width // LANES
    seg = tm // SUBLANES + 1

    @pl.when(pl.program_id(1) == 0)
    def _():
        x_scr[0:SUBLANES, :] = jnp.zeros((SUBLANES, width), F32)
        carry_scr[...] = jnp.zeros_like(carry_scr)

    h = h_ref[...]
    hn = _rms(h, g_ref[...]).astype(BF16)
    z = _dot(hn, win_ref[...])
    yb = z[:, width:]
    x_scr[SUBLANES:, :] = z[:, :width]
    cw = cw_ref[...]
    xb = cb_ref[...] + cw[3:4] * x_scr[SUBLANES:, :]
    for k in range(1, cw.shape[0]):
        xb = xb + cw[3 - k:4 - k] * x_scr[pl.ds(SUBLANES - k, tm), :]
    x_scr[0:SUBLANES, :] = x_scr[tm:, :]

    blk = width // LRU_BLOCKS
    xb16 = xb.astype(BF16)
    half_ga = jnp.concatenate([_dot(xb16[:, n * blk:(n + 1) * blk], wa_ref[n]) for n in range(LRU_BLOCKS)],
                              axis=-1) + ba_ref[...]
    half_gx = jnp.concatenate([_dot(xb16[:, n * blk:(n + 1) * blk], wx_ref[n]) for n in range(LRU_BLOCKS)],
                              axis=-1) + bx_ref[...]
    neg_lam = -lam_ref[...]
    softplus = jnp.maximum(neg_lam, 0.0) + jnp.log1p(jnp.exp(-jnp.abs(neg_lam)))
    half_c = (0.5 * LRU_C) * softplus
    neg_log_a = half_c * jnp.tanh(half_ga) + half_c
    a = jnp.exp2(neg_log_a * (-LOG2_E))
    one_minus_a2 = jnp.tanh(neg_log_a) * (1.0 + a * a)
    root = jnp.where(one_minus_a2 > 0.0, one_minus_a2 * lax.rsqrt(one_minus_a2), 0.0)
    b = root * ((0.5 * jnp.tanh(half_gx) + 0.5) * xb)

    for c in range(n_slabs):
        a_scr[c, 0:tm, :] = a[:, c * LANES:(c + 1) * LANES]
        b_scr[c, 0:tm, :] = b[:, c * LANES:(c + 1) * LANES]
        a_scr[c, tm:, :] = jnp.ones((SUBLANES * seg - tm, LANES), F32)
        b_scr[c, tm:, :] = jnp.zeros((SUBLANES * seg - tm, LANES), F32)

    def local_scan(j, carry):
        hl, pr = carry
        step = pl.ds(j, SUBLANES, stride=seg)
        out = pl.ds(pl.multiple_of(j * SUBLANES, SUBLANES), SUBLANES)
        hl_new, pr_new = [], []
        for c in range(n_slabs):
            a_j = a_scr[c, step, :]
            hl_c = a_j * hl[c] + b_scr[c, step, :]
            pr_c = a_j * pr[c]
            hl_scr[c, out, :] = hl_c
            pl_scr[c, out, :] = pr_c
            hl_new.append(hl_c)
            pr_new.append(pr_c)
        return tuple(hl_new), tuple(pr_new)

    zeros = tuple(jnp.zeros((SUBLANES, LANES), F32) for _ in range(n_slabs))
    ones = tuple(jnp.ones((SUBLANES, LANES), F32) for _ in range(n_slabs))
    hl_end, pr_end = lax.fori_loop(0, seg, local_scan, (zeros, ones), unroll=5)

    starts = []
    for c in range(n_slabs):
        state = carry_scr[c, 0:1, :]
        rows = []
        for s in range(SUBLANES):
            rows.append(state)
            state = pr_end[c][s:s + 1, :] * state + hl_end[c][s:s + 1, :]
        carry_scr[c, 0:1, :] = state
        starts.append(jnp.concatenate(rows, axis=0))

    def apply_carry(j, carry):
        step = pl.ds(j, SUBLANES, stride=seg)
        src = pl.ds(pl.multiple_of(j * SUBLANES, SUBLANES), SUBLANES)
        for c in range(n_slabs):
            a_scr[c, step, :] = hl_scr[c, src, :] + pl_scr[c, src, :] * starts[c]
        return carry

    lax.fori_loop(0, seg, apply_carry, 0, unroll=5)
    hs = jnp.concatenate([a_scr[c, 0:tm, :] for c in range(n_slabs)], axis=-1)

    y = (hs * jax.nn.gelu(yb)).astype(BF16)
    o_ref[...] = h + _dot(y, wout_ref[...])


def _recurrent(h, norm, w_in, conv_w, conv_b, wa, ba, wx, bx, lam, w_out, *, batch, seq):
    t, d = h.shape
    tm = TOKEN_TILE
    nts = seq // tm
    tok = pl.BlockSpec((tm, d), lambda b, j: (b * nts + j, 0))
    consts = [norm, w_in, conv_w, conv_b, wa, ba, wx, bx, lam, w_out]
    scan_rows = SUBLANES * (tm // SUBLANES + 1)
    scan_buf = pltpu.VMEM((d // LANES, scan_rows, LANES), F32)
    return pl.pallas_call(
        _rec_kernel,
        grid=(batch, nts),
        in_specs=[tok] + [_resident(c.shape) for c in consts],
        out_specs=tok,
        out_shape=jax.ShapeDtypeStruct((t, d), F32),
        scratch_shapes=[pltpu.VMEM((tm + SUBLANES, d), F32), scan_buf, scan_buf, scan_buf, scan_buf,
                        pltpu.VMEM((d // LANES, SUBLANES, LANES), F32)],
        compiler_params=pltpu.CompilerParams(dimension_semantics=("parallel", "arbitrary"),
                                             vmem_limit_bytes=VMEM_LIMIT),
        name="recurrent",
    )(h, *consts)


def kernel(x, p, rel_bias, ffn1_norm, ffn1_w_gate, ffn1_w_up, ffn1_w_down, mix_norm, hyb_w_in, hyb_conv_w, hyb_q_gain, hyb_k_gain, hyb_w_out, rec_w_in, rec_conv_w, rec_conv_b, lru_wa, lru_ba, lru_wx, lru_bx, lru_lambda, rec_w_out, ffn2_norm, ffn2_w_gate, ffn2_w_up, ffn2_w_down, ple_norm, ple_w_gate, ple_w_proj):
    batch, seq, d = x.shape
    depth = p.shape[0]
    t = batch * seq
    assert seq % ATTN_CHUNK == 0 and seq % TOKEN_TILE == 0
    seq_tiles = seq // TOKEN_TILE
    row = lambda a: a[None, :].astype(F32)
    w16 = lambda a: a.astype(BF16)

    bias_tab = _bias_table(rel_bias)
    h = x.reshape(t, d)
    for i in range(depth):
        h = _ffn(h, row(ffn1_norm[i]), w16(ffn1_w_gate[i]), w16(ffn1_w_up[i]), w16(ffn1_w_down[i]),
                 seq_tiles=seq_tiles)
        ple = (p[i].reshape(t, -1), row(ple_norm[i]), w16(ple_w_gate[i]), w16(ple_w_proj[i]))
        ffn2 = (row(ffn2_norm[i]), w16(ffn2_w_gate[i]), w16(ffn2_w_up[i]), w16(ffn2_w_down[i]))
        if i % 2 == 0:
            e = i // 2
            yc, q, k, v = _mix_in(h, row(mix_norm[i]), w16(hyb_w_in[e]), hyb_conv_w[e],
                                  hyb_q_gain[e], hyb_k_gain[e], batch=batch, seq=seq)
            ya = _attention(q, k, v, bias_tab)
            h = _ffn(h, *ffn2, mix=(yc, ya, w16(hyb_w_out[e])), ple=ple, seq_tiles=seq_tiles)
        else:
            o = i // 2
            h = _recurrent(h, row(mix_norm[i]), w16(rec_w_in[o]), rec_conv_w[o], row(rec_conv_b[o]),
                           w16(0.5 * lru_wa[o]), row(0.5 * lru_ba[o]), w16(0.5 * lru_wx[o]), row(0.5 * lru_bx[o]),
                           row(lru_lambda[o]), w16(rec_w_out[o]), batch=batch, seq=seq)
            h = _ffn(h, *ffn2, ple=ple, seq_tiles=seq_tiles)
    return h.reshape(batch, seq, d)
```

```python
import functools
import math

import jax
import jax.numpy as jnp
from jax import lax
from jax.experimental import pallas as pl
from jax.experimental.pallas import tpu as pltpu

F32 = jnp.float32
BF16 = jnp.bfloat16

EPS = 1e-6
HEAD_DIM = 64
N_HEADS = 8
ATTN_WIDTH = N_HEADS * HEAD_DIM
CONV_WIDTH = 512
DILATIONS = (1, 4, 16)
KEYS_BACK = 128
REL_BUCKETS = 32
REL_MAX_DIST = 2048
LRU_BLOCKS = 4
LRU_C = 8.0
LOG2_E = math.log2(math.e)
LANES = 128
SUBLANES = 8
N_SLABS = ATTN_WIDTH // LANES
ATTN_CHUNK = KEYS_BACK * DILATIONS[-1]
MASKED = -1e30
VMEM_LIMIT = 56 * 1024 * 1024

TOKEN_TILE = 512
FFN_WARMUP = 16
FFN_SUBTILES = 2


def _rms(x, gain):
    ms = jnp.mean(x * x, axis=-1, keepdims=True)
    return x * lax.rsqrt(ms + EPS) * gain


def _dot(a, b):
    return jnp.dot(a, b, preferred_element_type=F32)


def _resident(shape):
    nd = len(shape)
    return pl.BlockSpec(shape, lambda *_: (0,) * nd, pipeline_mode=pl.Buffered(1))


def _ffn_kernel(has_mix, has_ple, *refs):
    it = iter(refs)
    x_ref = next(it)
    if has_mix:
        yc_ref, ya_ref, wo_ref = next(it), next(it), next(it)
    g_ref, wg_ref, wu_ref, wd_ref = next(it), next(it), next(it), next(it)
    if has_ple:
        p_ref, pg_ref, pwg_ref, pwp_ref = next(it), next(it), next(it), next(it)
    o_ref = next(it)
    wg_s, wu_s, wd_s = next(it), next(it), next(it)
    casts = [(wg_ref, wg_s), (wu_ref, wu_s), (wd_ref, wd_s)]
    if has_mix:
        wo_s = next(it)
        casts.append((wo_ref, wo_s))
    if has_ple:
        pwg_s, pwp_s = next(it), next(it)
        casts += [(pwg_ref, pwg_s), (pwp_ref, pwp_s)]
    s = pl.program_id(0)

    @pl.when(s < FFN_WARMUP)
    def _():
        for src, dst in casts:
            rows = src.shape[0]
            dst[pl.ds(pl.multiple_of(s * rows, rows), rows), :] = src[...].astype(BF16)

    @pl.when(s >= FFN_WARMUP)
    def _():
        tm = TOKEN_TILE
        for sub in range(x_ref.shape[0] // tm):
            rows = slice(sub * tm, (sub + 1) * tm)
            x = x_ref[rows, :]
            if has_mix:
                ya = jnp.concatenate([ya_ref[j, rows, :] for j in range(N_SLABS)], axis=-1).astype(BF16)
                ymix = jnp.concatenate([yc_ref[rows, :], ya], axis=-1)
                x = x + _dot(ymix, wo_s[...])
            hn = _rms(x, g_ref[...]).astype(BF16)
            gate = _dot(hn, wg_s[...])
            up = _dot(hn, wu_s[...])
            act = (gate * jax.nn.sigmoid(gate) * up).astype(BF16)
            x = x + 0.5 * _dot(act, wd_s[...])
            if has_ple:
                hn2 = _rms(x, pg_ref[...]).astype(BF16)
                pgate = jax.nn.sigmoid(_dot(hn2, pwg_s[...]))
                x = x + pgate * _dot(p_ref[rows, :].astype(BF16), pwp_s[...])
            o_ref[rows, :] = x


def _ffn(h, layer, norm, w_gate, w_up, w_down, mix=None, ple=None, *, seq):
    t, d = h.shape
    tb = TOKEN_TILE * (1 if mix is not None else FFN_SUBTILES)
    seq_blocks = seq // tb
    warm = FFN_WARMUP
    blk = lambda s: jnp.maximum(s - warm, 0)
    tok = lambda s: (blk(s), 0)

    def weight(w, idx=layer):
        rows = w.shape[1] // warm
        return pl.BlockSpec((None, rows, w.shape[2]), lambda s: (idx, jnp.minimum(s, warm - 1), 0))

    def vec(g):
        return pl.BlockSpec((None, 1, g.shape[2]), lambda s: (layer, 0, 0))

    def resident(w):
        return pltpu.VMEM(w.shape[1:], BF16)

    args, specs = [h], [pl.BlockSpec((tb, d), tok)]
    scratch = [resident(w_gate), resident(w_up), resident(w_down)]
    if mix is not None:
        yc, ya, w_out, mix_layer = mix
        args += [yc, ya, w_out]
        specs += [pl.BlockSpec((tb, yc.shape[1]), tok),
                  pl.BlockSpec((None, N_SLABS, tb, LANES),
                               lambda s: (blk(s) // seq_blocks, 0, blk(s) % seq_blocks, 0)),
                  weight(w_out, mix_layer)]
        scratch.append(resident(w_out))
    args += [norm, w_gate, w_up, w_down]
    specs += [vec(norm), weight(w_gate), weight(w_up), weight(w_down)]
    if ple is not None:
        p, pnorm, pw_gate, pw_proj = ple
        args += [p, pnorm, pw_gate, pw_proj]
        specs += [pl.BlockSpec((None, tb, p.shape[2]), lambda s: (layer, blk(s), 0)), vec(pnorm),
                  weight(pw_gate), weight(pw_proj)]
        scratch += [resident(pw_gate), resident(pw_proj)]
    return pl.pallas_call(
        functools.partial(_ffn_kernel, mix is not None, ple is not None),
        grid=(warm + t // tb,),
        in_specs=specs,
        out_specs=pl.BlockSpec((tb, d), tok),
        out_shape=jax.ShapeDtypeStruct((t, d), F32),
        scratch_shapes=scratch,
        compiler_params=pltpu.CompilerParams(dimension_semantics=("arbitrary",), vmem_limit_bytes=VMEM_LIMIT),
        name="ffn" + ("_mix" if mix is not None else "") + ("_ple" if ple is not None else ""),
    )(*args)


def _shift_rows(u, prev, k):
    rolled = pltpu.roll(u, shift=k, axis=0)
    row = lax.broadcasted_iota(jnp.int32, u.shape, 0)
    for j in range(k):
        rolled = jnp.where(row == j, prev[8 - k + j:8 - k + j + 1, :], rolled)
    return rolled


def _head_mean_sq(x, blockdiag):
    x2 = x * x
    hi = x2.astype(BF16)
    lo = (x2 - hi.astype(F32)).astype(BF16)
    return _dot(hi, blockdiag) + _dot(lo, blockdiag)


def _mix_in_kernel(h_ref, g_ref, win_ref, cw_ref, qg_ref, kg_ref, bd_ref,
                   yc_ref, q_ref, k_ref, v_ref, carry_ref):
    @pl.when(pl.program_id(1) == 0)
    def _():
        carry_ref[...] = jnp.zeros_like(carry_ref)

    cwid = CONV_WIDTH
    hn = _rms(h_ref[...], g_ref[...]).astype(BF16)
    z = _dot(hn, win_ref[...])
    g_b, g_c, c_x = z[:, 0:cwid], z[:, cwid:2 * cwid], z[:, 2 * cwid:3 * cwid]
    u = g_c * c_x
    prev = carry_ref[...]
    cw = cw_ref[...]
    conv = cw[0:1] * _shift_rows(u, prev, 2) + cw[1:2] * _shift_rows(u, prev, 1) + cw[2:3] * u
    yc_ref[...] = (g_b * conv).astype(BF16)
    carry_ref[...] = u[u.shape[0] - 8:, :]

    q = z[:, 3 * cwid:3 * cwid + ATTN_WIDTH]
    k = z[:, 3 * cwid + ATTN_WIDTH:3 * cwid + 2 * ATTN_WIDTH]
    v = z[:, 3 * cwid + 2 * ATTN_WIDTH:]
    bd = bd_ref[...]
    qn = q * lax.rsqrt(_head_mean_sq(q, bd) + EPS) * (qg_ref[...] * (HEAD_DIM ** -0.5 * LOG2_E))
    kn = k * lax.rsqrt(_head_mean_sq(k, bd) + EPS) * kg_ref[...]
    for j in range(N_SLABS):
        q_ref[j] = qn[:, j * LANES:(j + 1) * LANES]
        k_ref[j] = kn[:, j * LANES:(j + 1) * LANES]
        v_ref[j] = v[:, j * LANES:(j + 1) * LANES]


def _mix_in(h, norm, w_in, conv_w, q_gain, k_gain, *, batch, seq):
    t, d = h.shape
    tm = TOKEN_TILE
    nts = seq // tm
    tok = lambda b, j: (b * nts + j, 0)
    slab = pl.BlockSpec((None, N_SLABS, tm, LANES), lambda b, j: (b, 0, j, 0))
    head_of = jnp.arange(ATTN_WIDTH) // HEAD_DIM
    blockdiag = jnp.where(head_of[:, None] == head_of[None, :], 1.0 / HEAD_DIM, 0.0).astype(BF16)
    qg = jnp.tile(q_gain, N_HEADS)[None, :]
    kg = jnp.tile(k_gain, N_HEADS)[None, :]
    slab_shape = jax.ShapeDtypeStruct((batch, N_SLABS, seq, LANES), F32)
    return pl.pallas_call(
        _mix_in_kernel,
        grid=(batch, nts),
        in_specs=[pl.BlockSpec((tm, d), tok), _resident(norm.shape), _resident(w_in.shape),
                  _resident(conv_w.shape), _resident(qg.shape), _resident(kg.shape), _resident(blockdiag.shape)],
        out_specs=[pl.BlockSpec((tm, CONV_WIDTH), tok), slab, slab, slab],
        out_shape=[jax.ShapeDtypeStruct((t, CONV_WIDTH), BF16), slab_shape, slab_shape, slab_shape],
        scratch_shapes=[pltpu.VMEM((8, CONV_WIDTH), F32)],
        compiler_params=pltpu.CompilerParams(dimension_semantics=("parallel", "arbitrary"),
                                             vmem_limit_bytes=VMEM_LIMIT),
        name="mix_in",
    )(h, norm, w_in, conv_w, qg, kg, blockdiag)


def _rel_bucket(dist):
    max_exact = REL_BUCKETS // 2
    n = jnp.maximum(dist, 1).astype(F32)
    large = max_exact + (jnp.log(n / max_exact) / math.log(REL_MAX_DIST / max_exact)
                         * (REL_BUCKETS - max_exact)).astype(jnp.int32)
    large = jnp.minimum(large, REL_BUCKETS - 1)
    return jnp.where(dist < max_exact, dist, large)


def _bias_table(rel_bias):
    qi = jnp.arange(KEYS_BACK)[:, None]
    kj = jnp.arange(2 * KEYS_BACK)[None, :]
    dist = qi + KEYS_BACK - kj
    valid = (dist >= 0) & (dist <= KEYS_BACK)
    tabs = []
    for d in DILATIONS:
        bucket = _rel_bucket(jnp.clip(dist, 0, KEYS_BACK) * d)
        onehot = (bucket[..., None] == jnp.arange(REL_BUCKETS)).astype(F32)
        bias = jnp.einsum("qkb,bh->hqk", onehot, rel_bias.astype(F32) * LOG2_E,
                          precision=lax.Precision.HIGHEST)
        tabs.append(jnp.stack([jnp.where(valid[None], bias, MASKED),
                               jnp.where((valid & (kj >= KEYS_BACK))[None], bias, MASKED)]))
    tab = jnp.stack(tabs)
    return tab.reshape(len(DILATIONS), 2, N_SLABS, 2 * KEYS_BACK, 2 * KEYS_BACK)


def _attn_kernel(q_ref, kp_ref, kc_ref, vp_ref, vc_ref, bias_ref, o_ref,
                 k_scr, v_scr, acc_scr, m_scr, l_scr):
    ch = ATTN_CHUNK
    k_scr[0:ch, :] = kp_ref[...]
    k_scr[ch:2 * ch, :] = kc_ref[...]
    v_scr[0:ch, :] = vp_ref[...]
    v_scr[ch:2 * ch, :] = vc_ref[...]
    first_chunk = pl.program_id(2) == 0
    head_a = lax.broadcasted_iota(jnp.int32, (1, LANES), 1) < HEAD_DIM
    ones = jnp.ones((2 * KEYS_BACK, LANES), BF16)
    n_units = ch // KEYS_BACK

    def unit(g, d, t):
        blocks = ch // (KEYS_BACK * d)
        rows = (lambda s, n: pl.ds(s, n)) if d == 1 else (lambda s, n: pl.ds(s, n, stride=d))
        r, i = t // blocks, t % blocks
        q_rows = rows(r + KEYS_BACK * d * i, KEYS_BACK)
        kv_rows = rows(ch + r + KEYS_BACK * d * (i - 1), 2 * KEYS_BACK)
        q = q_ref[q_rows, :]
        q2 = jnp.concatenate([jnp.where(head_a, q, 0.0), jnp.where(head_a, 0.0, q)], axis=0).astype(BF16)
        k = k_scr[kv_rows, :].astype(BF16)
        v = v_scr[kv_rows, :].astype(BF16)
        v_aug = jnp.concatenate([v, ones], axis=-1)
        variant = jnp.where(jnp.logical_and(first_chunk, i == 0), 1, 0)
        s = lax.dot_general(q2, k, (((1,), (1,)), ((), ())), preferred_element_type=F32)
        s = s + bias_ref[g, variant]
        m = jnp.max(s, axis=-1, keepdims=True)
        p = jnp.exp2(s - m).astype(BF16)
        pv = _dot(p, v_aug)
        nq = KEYS_BACK
        acc_scr[g, q_rows, :] = jnp.where(head_a, pv[:nq, :LANES], pv[nq:, :LANES])
        l_scr[g, q_rows, :] = jnp.where(head_a, pv[:nq, LANES:], pv[nq:, LANES:])
        m_scr[g, q_rows, :] = jnp.where(head_a, m[:nq], m[nq:])

    def all_branches(t, carry):
        for g, d in enumerate(DILATIONS):
            unit(g, d, t)
        return carry

    lax.fori_loop(0, n_units, all_branches, 0, unroll=4)

    def merge(t, carry):
        rows = pl.ds(pl.multiple_of(t * KEYS_BACK, KEYS_BACK), KEYS_BACK)
        ms = [m_scr[g, rows, :] for g in range(len(DILATIONS))]
        m_all = functools.reduce(jnp.maximum, ms)
        ws = [jnp.exp2(m - m_all) for m in ms]
        num = sum(w * acc_scr[g, rows, :] for g, w in enumerate(ws))
        den = sum(w * l_scr[g, rows, :] for g, w in enumerate(ws))
        o_ref[rows, :] = num / den
        return carry

    lax.fori_loop(0, n_units, merge, 0)


def _attention(q, k, v, bias_tab):
    batch, _, seq, _ = q.shape
    ch = ATTN_CHUNK
    nch = seq // ch
    nb = len(DILATIONS)
    cur = pl.BlockSpec((None, None, ch, LANES), lambda b, s, c: (b, s, c, 0))
    prv = pl.BlockSpec((None, None, ch, LANES), lambda b, s, c: (b, s, jnp.maximum(c - 1, 0), 0))
    bias_spec = pl.BlockSpec((nb, 2, None, 2 * KEYS_BACK, 2 * KEYS_BACK), lambda b, s, c: (0, 0, s, 0, 0))
    return pl.pallas_call(
        _attn_kernel,
        grid=(batch, N_SLABS, nch),
        in_specs=[cur, prv, cur, prv, cur, bias_spec],
        out_specs=cur,
        out_shape=jax.ShapeDtypeStruct(q.shape, F32),
        scratch_shapes=[pltpu.VMEM((2 * ch, LANES), F32), pltpu.VMEM((2 * ch, LANES), F32),
                        pltpu.VMEM((nb, ch, LANES), F32), pltpu.VMEM((nb, ch, LANES), F32),
                        pltpu.VMEM((nb, ch, LANES), F32)],
        compiler_params=pltpu.CompilerParams(dimension_semantics=("parallel", "parallel", "parallel"),
                                             vmem_limit_bytes=VMEM_LIMIT),
        name="dilated_attention",
    )(q, k, k, v, v, bias_tab)


def _rec_kernel(h_ref, g_ref, win_ref, cw_ref, cb_ref, wa_ref, ba_ref, wx_ref, bx_ref, lam_ref, wout_ref,
                o_ref, x_scr, a_scr, b_scr, hl_scr, pl_scr, carry_scr):
    tm, width = h_ref.shape
    n_slabs = width // LANES
    seg = tm // SUBLANES + 1

    @pl.when(pl.program_id(1) == 0)
    def _():
        x_scr[0:SUBLANES, :] = jnp.zeros((SUBLANES, width), F32)
        carry_scr[...] = jnp.zeros_like(carry_scr)

    h = h_ref[...]
    hn = _rms(h, g_ref[...]).astype(BF16)
    z = _dot(hn, win_ref[...])
    yb = z[:, width:]
    x_scr[SUBLANES:, :] = z[:, :width]
    cw = cw_ref[...]
    xb = cb_ref[...] + cw[3:4] * x_scr[SUBLANES:, :]
    for k in range(1, cw.shape[0]):
        xb = xb + cw[3 - k:4 - k] * x_scr[pl.ds(SUBLANES - k, tm), :]
    x_scr[0:SUBLANES, :] = x_scr[tm:, :]

    blk = width // LRU_BLOCKS
    xb16 = xb.astype(BF16)
    half_ga = jnp.concatenate([_dot(xb16[:, n * blk:(n + 1) * blk], wa_ref[n]) for n in range(LRU_BLOCKS)],
                              axis=-1) + ba_ref[...]
    half_gx = jnp.concatenate([_dot(xb16[:, n * blk:(n + 1) * blk], wx_ref[n]) for n in range(LRU_BLOCKS)],
                              axis=-1) + bx_ref[...]
    neg_lam = -lam_ref[...]
    softplus = jnp.maximum(neg_lam, 0.0) + jnp.log1p(jnp.exp(-jnp.abs(neg_lam)))
    half_c = (0.5 * LRU_C) * softplus
    neg_log_a = half_c * jnp.tanh(half_ga) + half_c
    a = jnp.exp2(neg_log_a * (-LOG2_E))
    one_minus_a2 = jnp.tanh(neg_log_a) * (1.0 + a * a)
    root = jnp.where(one_minus_a2 > 0.0, one_minus_a2 * lax.rsqrt(one_minus_a2), 0.0)
    b = root * ((0.5 * jnp.tanh(half_gx) + 0.5) * xb)

    for c in range(n_slabs):
        a_scr[c, 0:tm, :] = a[:, c * LANES:(c + 1) * LANES]
        b_scr[c, 0:tm, :] = b[:, c * LANES:(c + 1) * LANES]
        a_scr[c, tm:, :] = jnp.ones((SUBLANES * seg - tm, LANES), F32)
        b_scr[c, tm:, :] = jnp.zeros((SUBLANES * seg - tm, LANES), F32)

    def local_scan(j, carry):
        hl, pr = carry
        step = pl.ds(j, SUBLANES, stride=seg)
        out = pl.ds(pl.multiple_of(j * SUBLANES, SUBLANES), SUBLANES)
        hl_new, pr_new = [], []
        for c in range(n_slabs):
            a_j = a_scr[c, step, :]
            hl_c = a_j * hl[c] + b_scr[c, step, :]
            pr_c = a_j * pr[c]
            hl_scr[c, out, :] = hl_c
            pl_scr[c, out, :] = pr_c
            hl_new.append(hl_c)
            pr_new.append(pr_c)
        return tuple(hl_new), tuple(pr_new)

    zeros = tuple(jnp.zeros((SUBLANES, LANES), F32) for _ in range(n_slabs))
    ones = tuple(jnp.ones((SUBLANES, LANES), F32) for _ in range(n_slabs))
    hl_end, pr_end = lax.fori_loop(0, seg, local_scan, (zeros, ones), unroll=5)

    starts = []
    for c in range(n_slabs):
        state = carry_scr[c, 0:1, :]
        rows = []
        for s in range(SUBLANES):
            rows.append(state)
            state = pr_end[c][s:s + 1, :] * state + hl_end[c][s:s + 1, :]
        carry_scr[c, 0:1, :] = state
        starts.append(jnp.concatenate(rows, axis=0))

    def apply_carry(j, carry):
        step = pl.ds(j, SUBLANES, stride=seg)
        src = pl.ds(pl.multiple_of(j * SUBLANES, SUBLANES), SUBLANES)
        for c in range(n_slabs):
            a_scr[c, step, :] = hl_scr[c, src, :] + pl_scr[c, src, :] * starts[c]
        return carry

    lax.fori_loop(0, seg, apply_carry, 0, unroll=5)
    hs = jnp.concatenate([a_scr[c, 0:tm, :] for c in range(n_slabs)], axis=-1)

    y = (hs * jax.nn.gelu(yb)).astype(BF16)
    o_ref[...] = h + _dot(y, wout_ref[...])


def _recurrent(h, norm, w_in, conv_w, conv_b, wa, ba, wx, bx, lam, w_out, *, batch, seq):
    t, d = h.shape
    tm = TOKEN_TILE
    nts = seq // tm
    tok = pl.BlockSpec((tm, d), lambda b, j: (b * nts + j, 0))
    consts = [norm, w_in, conv_w, conv_b, wa, ba, wx, bx, lam, w_out]
    scan_rows = SUBLANES * (tm // SUBLANES + 1)
    scan_buf = pltpu.VMEM((d // LANES, scan_rows, LANES), F32)
    return pl.pallas_call(
        _rec_kernel,
        grid=(batch, nts),
        in_specs=[tok] + [_resident(c.shape) for c in consts],
        out_specs=tok,
        out_shape=jax.ShapeDtypeStruct((t, d), F32),
        scratch_shapes=[pltpu.VMEM((tm + SUBLANES, d), F32), scan_buf, scan_buf, scan_buf, scan_buf,
                        pltpu.VMEM((d // LANES, SUBLANES, LANES), F32)],
        compiler_params=pltpu.CompilerParams(dimension_semantics=("parallel", "arbitrary"),
                                             vmem_limit_bytes=VMEM_LIMIT),
        name="recurrent",
    )(h, *consts)


def kernel(x, p, rel_bias, ffn1_norm, ffn1_w_gate, ffn1_w_up, ffn1_w_down, mix_norm, hyb_w_in, hyb_conv_w, hyb_q_gain, hyb_k_gain, hyb_w_out, rec_w_in, rec_conv_w, rec_conv_b, lru_wa, lru_ba, lru_wx, lru_bx, lru_lambda, rec_w_out, ffn2_norm, ffn2_w_gate, ffn2_w_up, ffn2_w_down, ple_norm, ple_w_gate, ple_w_proj):
    batch, seq, d = x.shape
    depth = p.shape[0]
    t = batch * seq
    assert seq % ATTN_CHUNK == 0 and seq % (TOKEN_TILE * FFN_SUBTILES) == 0
    row = lambda a: a[None, :].astype(F32)
    w16 = lambda a: a.astype(BF16)

    bias_tab = _bias_table(rel_bias)
    h = x.reshape(t, d)
    rows = lambda a: a[:, None, :]
    ple = (p.reshape(depth, t, -1), rows(ple_norm), ple_w_gate, ple_w_proj)
    for i in range(depth):
        h = _ffn(h, i, rows(ffn1_norm), ffn1_w_gate, ffn1_w_up, ffn1_w_down, seq=seq)
        ffn2 = (i, rows(ffn2_norm), ffn2_w_gate, ffn2_w_up, ffn2_w_down)
        if i % 2 == 0:
            e = i // 2
            yc, q, k, v = _mix_in(h, row(mix_norm[i]), w16(hyb_w_in[e]), hyb_conv_w[e],
                                  hyb_q_gain[e], hyb_k_gain[e], batch=batch, seq=seq)
            ya = _attention(q, k, v, bias_tab)
            h = _ffn(h, *ffn2, mix=(yc, ya, hyb_w_out, e), ple=ple, seq=seq)
        else:
            o = i // 2
            h = _recurrent(h, row(mix_norm[i]), w16(rec_w_in[o]), rec_conv_w[o], row(rec_conv_b[o]),
                           w16(0.5 * lru_wa[o]), row(0.5 * lru_ba[o]), w16(0.5 * lru_wx[o]), row(0.5 * lru_bx[o]),
                           row(lru_lambda[o]), w16(rec_w_out[o]), batch=batch, seq=seq)
            h = _ffn(h, *ffn2, ple=ple, seq=seq)
    return h.reshape(batch, seq, d)
```

```python
import functools
import math

import jax
import jax.numpy as jnp
from jax import lax
from jax.experimental import pallas as pl
from jax.experimental.pallas import tpu as pltpu

F32 = jnp.float32
BF16 = jnp.bfloat16

EPS = 1e-6
HEAD_DIM = 64
N_HEADS = 8
ATTN_WIDTH = N_HEADS * HEAD_DIM
CONV_WIDTH = 512
DILATIONS = (1, 4, 16)
KEYS_BACK = 128
REL_BUCKETS = 32
REL_MAX_DIST = 2048
LRU_BLOCKS = 4
LRU_C = 8.0
LOG2_E = math.log2(math.e)
LANES = 128
SUBLANES = 8
N_SLABS = ATTN_WIDTH // LANES
ATTN_CHUNK = KEYS_BACK * DILATIONS[-1]
MASKED = -1e30
VMEM_LIMIT = 56 * 1024 * 1024

TOKEN_TILE = 512
FFN_WARMUP = 16
FFN_SUBTILES = 2


def _rms(x, gain):
    ms = jnp.mean(x * x, axis=-1, keepdims=True)
    return x * lax.rsqrt(ms + EPS) * gain


def _dot(a, b):
    return jnp.dot(a, b, preferred_element_type=F32)


def _resident(shape):
    nd = len(shape)
    return pl.BlockSpec(shape, lambda *_: (0,) * nd, pipeline_mode=pl.Buffered(1))


def _ffn_kernel(has_mix, has_ple, *refs):
    it = iter(refs)
    x_ref = next(it)
    if has_mix:
        yc_ref, ya_ref, wo_ref = next(it), next(it), next(it)
    g_ref, wg_ref, wu_ref, wd_ref = next(it), next(it), next(it), next(it)
    if has_ple:
        p_ref, pg_ref, pwg_ref, pwp_ref = next(it), next(it), next(it), next(it)
    o_ref = next(it)
    wg_s, wu_s, wd_s = next(it), next(it), next(it)
    casts = [(wg_ref, wg_s), (wu_ref, wu_s), (wd_ref, wd_s)]
    if has_mix:
        wo_s = next(it)
        casts.append((wo_ref, wo_s))
    if has_ple:
        pwg_s, pwp_s = next(it), next(it)
        casts += [(pwg_ref, pwg_s), (pwp_ref, pwp_s)]
    s = pl.program_id(0)

    @pl.when(s < FFN_WARMUP)
    def _():
        for src, dst in casts:
            rows = src.shape[0]
            dst[pl.ds(pl.multiple_of(s * rows, rows), rows), :] = src[...].astype(BF16)

    @pl.when(s >= FFN_WARMUP)
    def _():
        tm = TOKEN_TILE
        for sub in range(x_ref.shape[0] // tm):
            rows = slice(sub * tm, (sub + 1) * tm)
            x = x_ref[rows, :]
            if has_mix:
                ya = jnp.concatenate([ya_ref[j, rows, :] for j in range(N_SLABS)], axis=-1).astype(BF16)
                ymix = jnp.concatenate([yc_ref[rows, :], ya], axis=-1)
                x = x + _dot(ymix, wo_s[...])
            hn = _rms(x, g_ref[...]).astype(BF16)
            gate = _dot(hn, wg_s[...])
            up = _dot(hn, wu_s[...])
            act = (gate * jax.nn.sigmoid(gate) * up).astype(BF16)
            x = x + 0.5 * _dot(act, wd_s[...])
            if has_ple:
                hn2 = _rms(x, pg_ref[...]).astype(BF16)
                pgate = jax.nn.sigmoid(_dot(hn2, pwg_s[...]))
                x = x + pgate * _dot(p_ref[rows, :].astype(BF16), pwp_s[...])
            o_ref[rows, :] = x


def _ffn(h, layer, norm, w_gate, w_up, w_down, mix=None, ple=None, *, seq):
    t, d = h.shape
    tb = TOKEN_TILE * (1 if mix is not None else FFN_SUBTILES)
    seq_blocks = seq // tb
    warm = FFN_WARMUP
    blk = lambda s: jnp.maximum(s - warm, 0)
    tok = lambda s: (blk(s), 0)

    def weight(w, idx=layer):
        rows = w.shape[1] // warm
        return pl.BlockSpec((None, rows, w.shape[2]), lambda s: (idx, jnp.minimum(s, warm - 1), 0))

    def vec(g):
        return pl.BlockSpec((None, 1, g.shape[2]), lambda s: (layer, 0, 0))

    def resident(w):
        return pltpu.VMEM(w.shape[1:], BF16)

    args, specs = [h], [pl.BlockSpec((tb, d), tok)]
    scratch = [resident(w_gate), resident(w_up), resident(w_down)]
    if mix is not None:
        yc, ya, w_out, mix_layer = mix
        args += [yc, ya, w_out]
        specs += [pl.BlockSpec((tb, yc.shape[1]), tok),
                  pl.BlockSpec((None, N_SLABS, tb, LANES),
                               lambda s: (blk(s) // seq_blocks, 0, blk(s) % seq_blocks, 0)),
                  weight(w_out, mix_layer)]
        scratch.append(resident(w_out))
    args += [norm, w_gate, w_up, w_down]
    specs += [vec(norm), weight(w_gate), weight(w_up), weight(w_down)]
    if ple is not None:
        p, pnorm, pw_gate, pw_proj = ple
        args += [p, pnorm, pw_gate, pw_proj]
        specs += [pl.BlockSpec((None, tb, p.shape[2]), lambda s: (layer, blk(s), 0)), vec(pnorm),
                  weight(pw_gate), weight(pw_proj)]
        scratch += [resident(pw_gate), resident(pw_proj)]
    return pl.pallas_call(
        functools.partial(_ffn_kernel, mix is not None, ple is not None),
        grid=(warm + t // tb,),
        in_specs=specs,
        out_specs=pl.BlockSpec((tb, d), tok),
        out_shape=jax.ShapeDtypeStruct((t, d), F32),
        scratch_shapes=scratch,
        compiler_params=pltpu.CompilerParams(dimension_semantics=("arbitrary",), vmem_limit_bytes=VMEM_LIMIT),
        name="ffn" + ("_mix" if mix is not None else "") + ("_ple" if ple is not None else ""),
    )(*args)


def _shift_rows(u, prev, k):
    rolled = pltpu.roll(u, shift=k, axis=0)
    row = lax.broadcasted_iota(jnp.int32, u.shape, 0)
    for j in range(k):
        rolled = jnp.where(row == j, prev[8 - k + j:8 - k + j + 1, :], rolled)
    return rolled


def _head_rms(x, gain):
    head_a = lax.broadcasted_iota(jnp.int32, (1, LANES), 1) < HEAD_DIM
    outs = []
    for j in range(N_SLABS):
        xs = x[:, j * LANES:(j + 1) * LANES]
        x2 = xs * xs
        sum_a = jnp.sum(jnp.where(head_a, x2, 0.0), axis=-1, keepdims=True)
        sum_b = jnp.sum(jnp.where(head_a, 0.0, x2), axis=-1, keepdims=True)
        inv = lax.rsqrt(jnp.where(head_a, sum_a, sum_b) * (1.0 / HEAD_DIM) + EPS)
        outs.append(xs * inv * gain[:, j * LANES:(j + 1) * LANES])
    return outs


def _mix_in_kernel(h_ref, g_ref, win_ref, cw_ref, qg_ref, kg_ref,
                   yc_ref, q_ref, k_ref, v_ref, carry_ref):
    @pl.when(pl.program_id(1) == 0)
    def _():
        carry_ref[...] = jnp.zeros_like(carry_ref)

    cwid = CONV_WIDTH
    hn = _rms(h_ref[...], g_ref[...]).astype(BF16)
    z = _dot(hn, win_ref[...])
    g_b, g_c, c_x = z[:, 0:cwid], z[:, cwid:2 * cwid], z[:, 2 * cwid:3 * cwid]
    u = g_c * c_x
    prev = carry_ref[...]
    cw = cw_ref[...]
    conv = cw[0:1] * _shift_rows(u, prev, 2) + cw[1:2] * _shift_rows(u, prev, 1) + cw[2:3] * u
    yc_ref[...] = (g_b * conv).astype(BF16)
    carry_ref[...] = u[u.shape[0] - 8:, :]

    q = z[:, 3 * cwid:3 * cwid + ATTN_WIDTH]
    k = z[:, 3 * cwid + ATTN_WIDTH:3 * cwid + 2 * ATTN_WIDTH]
    v = z[:, 3 * cwid + 2 * ATTN_WIDTH:]
    qn = _head_rms(q, qg_ref[...] * (HEAD_DIM ** -0.5 * LOG2_E))
    kn = _head_rms(k, kg_ref[...])
    for j in range(N_SLABS):
        q_ref[j] = qn[j]
        k_ref[j] = kn[j]
        v_ref[j] = v[:, j * LANES:(j + 1) * LANES]


def _mix_in(h, norm, w_in, conv_w, q_gain, k_gain, *, batch, seq):
    t, d = h.shape
    tm = TOKEN_TILE
    nts = seq // tm
    tok = lambda b, j: (b * nts + j, 0)
    slab = pl.BlockSpec((None, N_SLABS, tm, LANES), lambda b, j: (b, 0, j, 0))
    qg = jnp.tile(q_gain, N_HEADS)[None, :]
    kg = jnp.tile(k_gain, N_HEADS)[None, :]
    slab_shape = jax.ShapeDtypeStruct((batch, N_SLABS, seq, LANES), F32)
    return pl.pallas_call(
        _mix_in_kernel,
        grid=(batch, nts),
        in_specs=[pl.BlockSpec((tm, d), tok), _resident(norm.shape), _resident(w_in.shape),
                  _resident(conv_w.shape), _resident(qg.shape), _resident(kg.shape)],
        out_specs=[pl.BlockSpec((tm, CONV_WIDTH), tok), slab, slab, slab],
        out_shape=[jax.ShapeDtypeStruct((t, CONV_WIDTH), BF16), slab_shape, slab_shape, slab_shape],
        scratch_shapes=[pltpu.VMEM((8, CONV_WIDTH), F32)],
        compiler_params=pltpu.CompilerParams(dimension_semantics=("parallel", "arbitrary"),
                                             vmem_limit_bytes=VMEM_LIMIT),
        name="mix_in",
    )(h, norm, w_in, conv_w, qg, kg)


def _rel_bucket(dist):
    max_exact = REL_BUCKETS // 2
    n = jnp.maximum(dist, 1).astype(F32)
    large = max_exact + (jnp.log(n / max_exact) / math.log(REL_MAX_DIST / max_exact)
                         * (REL_BUCKETS - max_exact)).astype(jnp.int32)
    large = jnp.minimum(large, REL_BUCKETS - 1)
    return jnp.where(dist < max_exact, dist, large)


def _bias_table(rel_bias):
    qi = jnp.arange(KEYS_BACK)[:, None]
    kj = jnp.arange(2 * KEYS_BACK)[None, :]
    dist = qi + KEYS_BACK - kj
    valid = (dist >= 0) & (dist <= KEYS_BACK)
    tabs = []
    for d in DILATIONS:
        bucket = _rel_bucket(jnp.clip(dist, 0, KEYS_BACK) * d)
        onehot = (bucket[..., None] == jnp.arange(REL_BUCKETS)).astype(F32)
        bias = jnp.einsum("qkb,bh->hqk", onehot, rel_bias.astype(F32) * LOG2_E,
                          precision=lax.Precision.HIGHEST)
        tabs.append(jnp.stack([jnp.where(valid[None], bias, MASKED),
                               jnp.where((valid & (kj >= KEYS_BACK))[None], bias, MASKED)]))
    tab = jnp.stack(tabs)
    return tab.reshape(len(DILATIONS), 2, N_SLABS, 2 * KEYS_BACK, 2 * KEYS_BACK)


def _attn_kernel(q_ref, kp_ref, kc_ref, vp_ref, vc_ref, bias_ref, o_ref, acc_scr, m_scr, l_scr):
    ch = ATTN_CHUNK
    nq = KEYS_BACK
    first_chunk = jnp.where(pl.program_id(2) == 0, 1, 0)
    head_a = lax.broadcasted_iota(jnp.int32, (1, LANES), 1) < HEAD_DIM
    ones = jnp.ones((2 * nq, LANES), BF16)

    def unit(g, d, t):
        blocks = ch // (nq * d)
        rows = (lambda s, n: pl.ds(s, n)) if d == 1 else (lambda s, n: pl.ds(s, n, stride=d))
        r, i = t // blocks, t % blocks
        q_rows = rows(r + nq * d * i, nq)
        if i == 0:
            lo, hi = rows(ch - nq * d + r, nq), rows(r, nq)
            k = jnp.concatenate([kp_ref[lo, :], kc_ref[hi, :]], axis=0)
            v = jnp.concatenate([vp_ref[lo, :], vc_ref[hi, :]], axis=0)
            bias = bias_ref[g, first_chunk]
        else:
            kv_rows = rows(r + nq * d * (i - 1), 2 * nq)
            k, v = kc_ref[kv_rows, :], vc_ref[kv_rows, :]
            bias = bias_ref[g, 0]
        q = q_ref[q_rows, :]
        q2 = jnp.concatenate([jnp.where(head_a, q, 0.0), jnp.where(head_a, 0.0, q)], axis=0).astype(BF16)
        v_aug = jnp.concatenate([v.astype(BF16), ones], axis=-1)
        s = lax.dot_general(q2, k.astype(BF16), (((1,), (1,)), ((), ())), preferred_element_type=F32)
        s = s + bias
        m = jnp.max(s, axis=-1, keepdims=True)
        p = jnp.exp2(s - m).astype(BF16)
        pv = _dot(p, v_aug)
        acc_scr[g, q_rows, :] = jnp.where(head_a, pv[:nq, :LANES], pv[nq:, :LANES])
        l_scr[g, q_rows, :] = jnp.where(head_a, pv[:nq, LANES:], pv[nq:, LANES:])
        m_scr[g, q_rows, :] = jnp.where(head_a, m[:nq], m[nq:])

    for t in range(ch // nq):
        for g, d in enumerate(DILATIONS):
            unit(g, d, t)

    def merge(t, carry):
        rows = pl.ds(pl.multiple_of(t * KEYS_BACK, KEYS_BACK), KEYS_BACK)
        ms = [m_scr[g, rows, :] for g in range(len(DILATIONS))]
        m_all = functools.reduce(jnp.maximum, ms)
        ws = [jnp.exp2(m - m_all) for m in ms]
        num = sum(w * acc_scr[g, rows, :] for g, w in enumerate(ws))
        den = sum(w * l_scr[g, rows, :] for g, w in enumerate(ws))
        o_ref[rows, :] = num / den
        return carry

    lax.fori_loop(0, ch // nq, merge, 0, unroll=2)


def _attention(q, k, v, bias_tab):
    batch, _, seq, _ = q.shape
    ch = ATTN_CHUNK
    nch = seq // ch
    nb = len(DILATIONS)
    cur = pl.BlockSpec((None, None, ch, LANES), lambda b, s, c: (b, s, c, 0))
    prv = pl.BlockSpec((None, None, ch, LANES), lambda b, s, c: (b, s, jnp.maximum(c - 1, 0), 0))
    bias_spec = pl.BlockSpec((nb, 2, None, 2 * KEYS_BACK, 2 * KEYS_BACK), lambda b, s, c: (0, 0, s, 0, 0))
    return pl.pallas_call(
        _attn_kernel,
        grid=(batch, N_SLABS, nch),
        in_specs=[cur, prv, cur, prv, cur, bias_spec],
        out_specs=cur,
        out_shape=jax.ShapeDtypeStruct(q.shape, F32),
        scratch_shapes=[pltpu.VMEM((nb, ch, LANES), F32), pltpu.VMEM((nb, ch, LANES), F32),
                        pltpu.VMEM((nb, ch, LANES), F32)],
        compiler_params=pltpu.CompilerParams(dimension_semantics=("parallel", "parallel", "parallel"),
                                             vmem_limit_bytes=VMEM_LIMIT),
        name="dilated_attention",
    )(q, k, k, v, v, bias_tab)


def _rec_kernel(h_ref, g_ref, win_ref, cw_ref, cb_ref, wa_ref, ba_ref, wx_ref, bx_ref, lam_ref, wout_ref,
                o_ref, x_scr, a_scr, b_scr, hl_scr, pl_scr, carry_scr):
    tm, width = h_ref.shape
    n_slabs = width // LANES
    seg = tm // SUBLANES + 1

    @pl.when(pl.program_id(1) == 0)
    def _():
        x_scr[0:SUBLANES, :] = jnp.zeros((SUBLANES, width), F32)
        carry_scr[...] = jnp.zeros_like(carry_scr)

    h = h_ref[...]
    hn = _rms(h, g_ref[...]).astype(BF16)
    z = _dot(hn, win_ref[...])
    yb = z[:, width:]
    x_scr[SUBLANES:, :] = z[:, :width]
    cw = cw_ref[...]
    xb = cb_ref[...] + cw[3:4] * x_scr[SUBLANES:, :]
    for k in range(1, cw.shape[0]):
        xb = xb + cw[3 - k:4 - k] * x_scr[pl.ds(SUBLANES - k, tm), :]
    x_scr[0:SUBLANES, :] = x_scr[tm:, :]

    blk = width // LRU_BLOCKS
    xb16 = xb.astype(BF16)
    half_ga = jnp.concatenate([_dot(xb16[:, n * blk:(n + 1) * blk], wa_ref[n]) for n in range(LRU_BLOCKS)],
                              axis=-1) + ba_ref[...]
    half_gx = jnp.concatenate([_dot(xb16[:, n * blk:(n + 1) * blk], wx_ref[n]) for n in range(LRU_BLOCKS)],
                              axis=-1) + bx_ref[...]
    neg_lam = -lam_ref[...]
    softplus = jnp.maximum(neg_lam, 0.0) + jnp.log1p(jnp.exp(-jnp.abs(neg_lam)))
    half_c = (0.5 * LRU_C) * softplus
    neg_log_a = half_c * jnp.tanh(half_ga) + half_c
    a = jnp.exp2(neg_log_a * (-LOG2_E))
    one_minus_a2 = jnp.tanh(neg_log_a) * (1.0 + a * a)
    root = jnp.where(one_minus_a2 > 0.0, one_minus_a2 * lax.rsqrt(one_minus_a2), 0.0)
    b = root * ((0.5 * jnp.tanh(half_gx) + 0.5) * xb)

    for c in range(n_slabs):
        a_scr[c, 0:tm, :] = a[:, c * LANES:(c + 1) * LANES]
        b_scr[c, 0:tm, :] = b[:, c * LANES:(c + 1) * LANES]
        a_scr[c, tm:, :] = jnp.ones((SUBLANES * seg - tm, LANES), F32)
        b_scr[c, tm:, :] = jnp.zeros((SUBLANES * seg - tm, LANES), F32)

    def local_scan(j, carry):
        hl, pr = carry
        step = pl.ds(j, SUBLANES, stride=seg)
        out = pl.ds(pl.multiple_of(j * SUBLANES, SUBLANES), SUBLANES)
        hl_new, pr_new = [], []
        for c in range(n_slabs):
            a_j = a_scr[c, step, :]
            hl_c = a_j * hl[c] + b_scr[c, step, :]
            pr_c = a_j * pr[c]
            hl_scr[c, out, :] = hl_c
            pl_scr[c, out, :] = pr_c
            hl_new.append(hl_c)
            pr_new.append(pr_c)
        return tuple(hl_new), tuple(pr_new)

    zeros = tuple(jnp.zeros((SUBLANES, LANES), F32) for _ in range(n_slabs))
    ones = tuple(jnp.ones((SUBLANES, LANES), F32) for _ in range(n_slabs))
    hl_end, pr_end = lax.fori_loop(0, seg, local_scan, (zeros, ones), unroll=5)

    starts = []
    for c in range(n_slabs):
        state = carry_scr[c, 0:1, :]
        rows = []
        for s in range(SUBLANES):
            rows.append(state)
            state = pr_end[c][s:s + 1, :] * state + hl_end[c][s:s + 1, :]
        carry_scr[c, 0:1, :] = state
        starts.append(jnp.concatenate(rows, axis=0))

    def apply_carry(j, carry):
        step = pl.ds(j, SUBLANES, stride=seg)
        src = pl.ds(pl.multiple_of(j * SUBLANES, SUBLANES), SUBLANES)
        for c in range(n_slabs):
            a_scr[c, step, :] = hl_scr[c, src, :] + pl_scr[c, src, :] * starts[c]
        return carry

    lax.fori_loop(0, seg, apply_carry, 0, unroll=5)
    hs = jnp.concatenate([a_scr[c, 0:tm, :] for c in range(n_slabs)], axis=-1)

    y = (hs * jax.nn.gelu(yb)).astype(BF16)
    o_ref[...] = h + _dot(y, wout_ref[...])


def _recurrent(h, norm, w_in, conv_w, conv_b, wa, ba, wx, bx, lam, w_out, *, batch, seq):
    t, d = h.shape
    tm = TOKEN_TILE
    nts = seq // tm
    tok = pl.BlockSpec((tm, d), lambda b, j: (b * nts + j, 0))
    consts = [norm, w_in, conv_w, conv_b, wa, ba, wx, bx, lam, w_out]
    scan_rows = SUBLANES * (tm // SUBLANES + 1)
    scan_buf = pltpu.VMEM((d // LANES, scan_rows, LANES), F32)
    return pl.pallas_call(
        _rec_kernel,
        grid=(batch, nts),
        in_specs=[tok] + [_resident(c.shape) for c in consts],
        out_specs=tok,
        out_shape=jax.ShapeDtypeStruct((t, d), F32),
        scratch_shapes=[pltpu.VMEM((tm + SUBLANES, d), F32), scan_buf, scan_buf, scan_buf, scan_buf,
                        pltpu.VMEM((d // LANES, SUBLANES, LANES), F32)],
        compiler_params=pltpu.CompilerParams(dimension_semantics=("parallel", "arbitrary"),
                                             vmem_limit_bytes=VMEM_LIMIT),
        name="recurrent",
    )(h, *consts)


def kernel(x, p, rel_bias, ffn1_norm, ffn1_w_gate, ffn1_w_up, ffn1_w_down, mix_norm, hyb_w_in, hyb_conv_w, hyb_q_gain, hyb_k_gain, hyb_w_out, rec_w_in, rec_conv_w, rec_conv_b, lru_wa, lru_ba, lru_wx, lru_bx, lru_lambda, rec_w_out, ffn2_norm, ffn2_w_gate, ffn2_w_up, ffn2_w_down, ple_norm, ple_w_gate, ple_w_proj):
    batch, seq, d = x.shape
    depth = p.shape[0]
    t = batch * seq
    assert seq % ATTN_CHUNK == 0 and seq % (TOKEN_TILE * FFN_SUBTILES) == 0
    row = lambda a: a[None, :].astype(F32)
    w16 = lambda a: a.astype(BF16)

    bias_tab = _bias_table(rel_bias)
    h = x.reshape(t, d)
    rows = lambda a: a[:, None, :]
    ple = (p.reshape(depth, t, -1), rows(ple_norm), ple_w_gate, ple_w_proj)
    for i in range(depth):
        h = _ffn(h, i, rows(ffn1_norm), ffn1_w_gate, ffn1_w_up, ffn1_w_down, seq=seq)
        ffn2 = (i, rows(ffn2_norm), ffn2_w_gate, ffn2_w_up, ffn2_w_down)
        if i % 2 == 0:
            e = i // 2
            yc, q, k, v = _mix_in(h, row(mix_norm[i]), w16(hyb_w_in[e]), hyb_conv_w[e],
                                  hyb_q_gain[e], hyb_k_gain[e], batch=batch, seq=seq)
            ya = _attention(q, k, v, bias_tab)
            h = _ffn(h, *ffn2, mix=(yc, ya, hyb_w_out, e), ple=ple, seq=seq)
        else:
            o = i // 2
            h = _recurrent(h, row(mix_norm[i]), w16(rec_w_in[o]), rec_conv_w[o], row(rec_conv_b[o]),
                           w16(0.5 * lru_wa[o]), row(0.5 * lru_ba[o]), w16(0.5 * lru_wx[o]), row(0.5 * lru_bx[o]),
                           row(lru_lambda[o]), w16(rec_w_out[o]), batch=batch, seq=seq)
            h = _ffn(h, *ffn2, ple=ple, seq=seq)
    return h.reshape(batch, seq, d)
```

```python
import functools
import math

import jax
import jax.numpy as jnp
from jax import lax
from jax.experimental import pallas as pl
from jax.experimental.pallas import tpu as pltpu

F32 = jnp.float32
BF16 = jnp.bfloat16

EPS = 1e-6
HEAD_DIM = 64
N_HEADS = 8
ATTN_WIDTH = N_HEADS * HEAD_DIM
CONV_WIDTH = 512
DILATIONS = (1, 4, 16)
KEYS_BACK = 128
REL_BUCKETS = 32
REL_MAX_DIST = 2048
LRU_BLOCKS = 4
LRU_C = 8.0
LOG2_E = math.log2(math.e)
LANES = 128
SUBLANES = 8
N_SLABS = ATTN_WIDTH // LANES
ATTN_CHUNK = KEYS_BACK * DILATIONS[-1]
MASKED = -1e30
VMEM_LIMIT = 56 * 1024 * 1024

TOKEN_TILE = 512
FFN_WARMUP = 16
FFN_SUBTILES = 2
REC_LANES = 2


def _rms(x, gain):
    ms = jnp.mean(x * x, axis=-1, keepdims=True)
    return x * lax.rsqrt(ms + EPS) * gain


def _dot(a, b):
    return jnp.dot(a, b, preferred_element_type=F32)


def _resident(shape):
    nd = len(shape)
    return pl.BlockSpec(shape, lambda *_: (0,) * nd, pipeline_mode=pl.Buffered(1))


def _ffn_kernel(has_mix, has_ple, *refs):
    it = iter(refs)
    x_ref = next(it)
    if has_mix:
        yc_ref, ya_ref, wo_ref = next(it), next(it), next(it)
    g_ref, wg_ref, wu_ref, wd_ref = next(it), next(it), next(it), next(it)
    if has_ple:
        p_ref, pg_ref, pwg_ref, pwp_ref = next(it), next(it), next(it), next(it)
    o_ref = next(it)
    wg_s, wu_s, wd_s = next(it), next(it), next(it)
    casts = [(wg_ref, wg_s), (wu_ref, wu_s), (wd_ref, wd_s)]
    if has_mix:
        wo_s = next(it)
        casts.append((wo_ref, wo_s))
    if has_ple:
        pwg_s, pwp_s = next(it), next(it)
        casts += [(pwg_ref, pwg_s), (pwp_ref, pwp_s)]
    s = pl.program_id(0)

    @pl.when(s < FFN_WARMUP)
    def _():
        for src, dst in casts:
            rows = src.shape[0]
            dst[pl.ds(pl.multiple_of(s * rows, rows), rows), :] = src[...].astype(BF16)

    @pl.when(s >= FFN_WARMUP)
    def _():
        tm = TOKEN_TILE
        for sub in range(x_ref.shape[0] // tm):
            rows = slice(sub * tm, (sub + 1) * tm)
            x = x_ref[rows, :]
            if has_mix:
                ya = jnp.concatenate([ya_ref[j, rows, :] for j in range(N_SLABS)], axis=-1).astype(BF16)
                ymix = jnp.concatenate([yc_ref[rows, :], ya], axis=-1)
                x = x + _dot(ymix, wo_s[...])
            hn = _rms(x, g_ref[...]).astype(BF16)
            gate = _dot(hn, wg_s[...])
            up = _dot(hn, wu_s[...])
            act = (gate * jax.nn.sigmoid(gate) * up).astype(BF16)
            x = x + 0.5 * _dot(act, wd_s[...])
            if has_ple:
                hn2 = _rms(x, pg_ref[...]).astype(BF16)
                pgate = jax.nn.sigmoid(_dot(hn2, pwg_s[...]))
                x = x + pgate * _dot(p_ref[rows, :].astype(BF16), pwp_s[...])
            o_ref[rows, :] = x


def _ffn(h, layer, norm, w_gate, w_up, w_down, mix=None, ple=None, *, seq):
    t, d = h.shape
    tb = TOKEN_TILE * (1 if mix is not None else FFN_SUBTILES)
    seq_blocks = seq // tb
    warm = FFN_WARMUP
    blk = lambda s: jnp.maximum(s - warm, 0)
    tok = lambda s: (blk(s), 0)

    def weight(w, idx=layer):
        rows = w.shape[1] // warm
        return pl.BlockSpec((None, rows, w.shape[2]), lambda s: (idx, jnp.minimum(s, warm - 1), 0))

    def vec(g):
        return pl.BlockSpec((None, 1, g.shape[2]), lambda s: (layer, 0, 0))

    def resident(w):
        return pltpu.VMEM(w.shape[1:], BF16)

    args, specs = [h], [pl.BlockSpec((tb, d), tok)]
    scratch = [resident(w_gate), resident(w_up), resident(w_down)]
    if mix is not None:
        yc, ya, w_out, mix_layer = mix
        args += [yc, ya, w_out]
        specs += [pl.BlockSpec((tb, yc.shape[1]), tok),
                  pl.BlockSpec((None, N_SLABS, tb, LANES),
                               lambda s: (blk(s) // seq_blocks, 0, blk(s) % seq_blocks, 0)),
                  weight(w_out, mix_layer)]
        scratch.append(resident(w_out))
    args += [norm, w_gate, w_up, w_down]
    specs += [vec(norm), weight(w_gate), weight(w_up), weight(w_down)]
    if ple is not None:
        p, pnorm, pw_gate, pw_proj = ple
        args += [p, pnorm, pw_gate, pw_proj]
        specs += [pl.BlockSpec((None, tb, p.shape[2]), lambda s: (layer, blk(s), 0)), vec(pnorm),
                  weight(pw_gate), weight(pw_proj)]
        scratch += [resident(pw_gate), resident(pw_proj)]
    return pl.pallas_call(
        functools.partial(_ffn_kernel, mix is not None, ple is not None),
        grid=(warm + t // tb,),
        in_specs=specs,
        out_specs=pl.BlockSpec((tb, d), tok),
        out_shape=jax.ShapeDtypeStruct((t, d), F32),
        scratch_shapes=scratch,
        compiler_params=pltpu.CompilerParams(dimension_semantics=("arbitrary",), vmem_limit_bytes=VMEM_LIMIT),
        name="ffn" + ("_mix" if mix is not None else "") + ("_ple" if ple is not None else ""),
    )(*args)


def _shift_rows(u, prev, k):
    rolled = pltpu.roll(u, shift=k, axis=0)
    row = lax.broadcasted_iota(jnp.int32, u.shape, 0)
    for j in range(k):
        rolled = jnp.where(row == j, prev[8 - k + j:8 - k + j + 1, :], rolled)
    return rolled


def _head_rms(x, gain):
    head_a = lax.broadcasted_iota(jnp.int32, (1, LANES), 1) < HEAD_DIM
    outs = []
    for j in range(N_SLABS):
        xs = x[:, j * LANES:(j + 1) * LANES]
        x2 = xs * xs
        sum_a = jnp.sum(jnp.where(head_a, x2, 0.0), axis=-1, keepdims=True)
        sum_b = jnp.sum(jnp.where(head_a, 0.0, x2), axis=-1, keepdims=True)
        inv = lax.rsqrt(jnp.where(head_a, sum_a, sum_b) * (1.0 / HEAD_DIM) + EPS)
        outs.append(xs * inv * gain[:, j * LANES:(j + 1) * LANES])
    return outs


def _mix_in_kernel(h_ref, g_ref, win_ref, cw_ref, qg_ref, kg_ref,
                   yc_ref, q_ref, k_ref, v_ref, carry_ref):
    @pl.when(pl.program_id(1) == 0)
    def _():
        carry_ref[...] = jnp.zeros_like(carry_ref)

    cwid = CONV_WIDTH
    hn = _rms(h_ref[...], g_ref[...]).astype(BF16)
    z = _dot(hn, win_ref[...])
    g_b, g_c, c_x = z[:, 0:cwid], z[:, cwid:2 * cwid], z[:, 2 * cwid:3 * cwid]
    u = g_c * c_x
    prev = carry_ref[...]
    cw = cw_ref[...]
    conv = cw[0:1] * _shift_rows(u, prev, 2) + cw[1:2] * _shift_rows(u, prev, 1) + cw[2:3] * u
    yc_ref[...] = (g_b * conv).astype(BF16)
    carry_ref[...] = u[u.shape[0] - 8:, :]

    q = z[:, 3 * cwid:3 * cwid + ATTN_WIDTH]
    k = z[:, 3 * cwid + ATTN_WIDTH:3 * cwid + 2 * ATTN_WIDTH]
    v = z[:, 3 * cwid + 2 * ATTN_WIDTH:]
    qn = _head_rms(q, qg_ref[...] * (HEAD_DIM ** -0.5 * LOG2_E))
    kn = _head_rms(k, kg_ref[...])
    for j in range(N_SLABS):
        q_ref[j] = qn[j]
        k_ref[j] = kn[j]
        v_ref[j] = v[:, j * LANES:(j + 1) * LANES]


def _mix_in(h, norm, w_in, conv_w, q_gain, k_gain, *, batch, seq):
    t, d = h.shape
    tm = TOKEN_TILE
    nts = seq // tm
    tok = lambda b, j: (b * nts + j, 0)
    slab = pl.BlockSpec((None, N_SLABS, tm, LANES), lambda b, j: (b, 0, j, 0))
    qg = jnp.tile(q_gain, N_HEADS)[None, :]
    kg = jnp.tile(k_gain, N_HEADS)[None, :]
    slab_shape = jax.ShapeDtypeStruct((batch, N_SLABS, seq, LANES), F32)
    return pl.pallas_call(
        _mix_in_kernel,
        grid=(batch, nts),
        in_specs=[pl.BlockSpec((tm, d), tok), _resident(norm.shape), _resident(w_in.shape),
                  _resident(conv_w.shape), _resident(qg.shape), _resident(kg.shape)],
        out_specs=[pl.BlockSpec((tm, CONV_WIDTH), tok), slab, slab, slab],
        out_shape=[jax.ShapeDtypeStruct((t, CONV_WIDTH), BF16), slab_shape, slab_shape, slab_shape],
        scratch_shapes=[pltpu.VMEM((8, CONV_WIDTH), F32)],
        compiler_params=pltpu.CompilerParams(dimension_semantics=("parallel", "arbitrary"),
                                             vmem_limit_bytes=VMEM_LIMIT),
        name="mix_in",
    )(h, norm, w_in, conv_w, qg, kg)


def _rel_bucket(dist):
    max_exact = REL_BUCKETS // 2
    n = jnp.maximum(dist, 1).astype(F32)
    large = max_exact + (jnp.log(n / max_exact) / math.log(REL_MAX_DIST / max_exact)
                         * (REL_BUCKETS - max_exact)).astype(jnp.int32)
    large = jnp.minimum(large, REL_BUCKETS - 1)
    return jnp.where(dist < max_exact, dist, large)


def _bias_table(rel_bias):
    qi = jnp.arange(KEYS_BACK)[:, None]
    kj = jnp.arange(2 * KEYS_BACK)[None, :]
    dist = qi + KEYS_BACK - kj
    valid = (dist >= 0) & (dist <= KEYS_BACK)
    tabs = []
    for d in DILATIONS:
        bucket = _rel_bucket(jnp.clip(dist, 0, KEYS_BACK) * d)
        onehot = (bucket[..., None] == jnp.arange(REL_BUCKETS)).astype(F32)
        bias = jnp.einsum("qkb,bh->hqk", onehot, rel_bias.astype(F32) * LOG2_E,
                          precision=lax.Precision.HIGHEST)
        tabs.append(jnp.stack([jnp.where(valid[None], bias, MASKED),
                               jnp.where((valid & (kj >= KEYS_BACK))[None], bias, MASKED)]))
    tab = jnp.stack(tabs)
    return tab.reshape(len(DILATIONS), 2, N_SLABS, 2 * KEYS_BACK, 2 * KEYS_BACK)


def _attn_kernel(q_ref, kp_ref, kc_ref, vp_ref, vc_ref, bias_ref, o_ref,
                 q4_scr, kp4_scr, kc4_scr, vp4_scr, vc4_scr, acc_scr, m_scr, l_scr):
    ch = ATTN_CHUNK
    nq = KEYS_BACK
    quarter = ch // 4
    first_chunk = jnp.where(pl.program_id(2) == 0, 1, 0)
    head_a = lax.broadcasted_iota(jnp.int32, (1, LANES), 1) < HEAD_DIM
    ones = jnp.ones((2 * nq, LANES), BF16)

    for src, dst in ((q_ref, q4_scr), (kp_ref, kp4_scr), (kc_ref, kc4_scr), (vp_ref, vp4_scr), (vc_ref, vc4_scr)):
        for r in range(4):
            dst[r * quarter:(r + 1) * quarter, :] = src[pl.ds(r, quarter, stride=4), :]

    def attend(g, q, k, v, bias, out_rows):
        q2 = jnp.concatenate([jnp.where(head_a, q, 0.0), jnp.where(head_a, 0.0, q)], axis=0).astype(BF16)
        v_aug = jnp.concatenate([v.astype(BF16), ones], axis=-1)
        s = lax.dot_general(q2, k.astype(BF16), (((1,), (1,)), ((), ())), preferred_element_type=F32)
        s = s + bias
        m = jnp.max(s, axis=-1, keepdims=True)
        p = jnp.exp2(s - m).astype(BF16)
        pv = _dot(p, v_aug)
        acc_scr[g, out_rows, :] = jnp.where(head_a, pv[:nq, :LANES], pv[nq:, :LANES])
        l_scr[g, out_rows, :] = jnp.where(head_a, pv[:nq, LANES:], pv[nq:, LANES:])
        m_scr[g, out_rows, :] = jnp.where(head_a, m[:nq], m[nq:])

    def cat(lo, hi):
        return jnp.concatenate([lo, hi], axis=0)

    def unit(g, t):
        first = bias_ref[g, first_chunk]
        if g == 0:
            q_rows = pl.ds(t * nq, nq)
            if t == 0:
                lo, hi = pl.ds(ch - nq, nq), pl.ds(0, nq)
                attend(g, q_ref[q_rows, :], cat(kp_ref[lo, :], kc_ref[hi, :]), cat(vp_ref[lo, :], vc_ref[hi, :]),
                       first, q_rows)
            else:
                kv = pl.ds((t - 1) * nq, 2 * nq)
                attend(g, q_ref[q_rows, :], kc_ref[kv, :], vc_ref[kv, :], bias_ref[g, 0], q_rows)
        elif g == 1:
            r, i = t // 4, t % 4
            q_rows = pl.ds(r * quarter + i * nq, nq)
            if i == 0:
                lo, hi = pl.ds((r + 1) * quarter - nq, nq), pl.ds(r * quarter, nq)
                attend(g, q4_scr[q_rows, :], cat(kp4_scr[lo, :], kc4_scr[hi, :]),
                       cat(vp4_scr[lo, :], vc4_scr[hi, :]), first, q_rows)
            else:
                kv = pl.ds(r * quarter + (i - 1) * nq, 2 * nq)
                attend(g, q4_scr[q_rows, :], kc4_scr[kv, :], vc4_scr[kv, :], bias_ref[g, 0], q_rows)
        else:
            rows = pl.ds((t % 4) * quarter + t // 4, nq, stride=4)
            attend(g, q4_scr[rows, :], cat(kp4_scr[rows, :], kc4_scr[rows, :]),
                   cat(vp4_scr[rows, :], vc4_scr[rows, :]), first, rows)

    for t in range(ch // nq):
        for g in range(len(DILATIONS)):
            unit(g, t)

    def merge(t, carry):
        r, blk = t // 4, t % 4
        rows4 = pl.ds(pl.multiple_of(t * nq, nq), nq)
        rows0 = pl.ds(r + blk * (4 * nq), nq, stride=4)
        sel = lambda scr: [scr[0, rows0, :], scr[1, rows4, :], scr[2, rows4, :]]
        ms = sel(m_scr)
        m_all = functools.reduce(jnp.maximum, ms)
        ws = [jnp.exp2(m - m_all) for m in ms]
        num = sum(w * a for w, a in zip(ws, sel(acc_scr)))
        den = sum(w * l for w, l in zip(ws, sel(l_scr)))
        o_ref[rows0, :] = num / den
        return carry

    lax.fori_loop(0, ch // nq, merge, 0, unroll=2)


def _attention(q, k, v, bias_tab):
    batch, _, seq, _ = q.shape
    ch = ATTN_CHUNK
    nch = seq // ch
    nb = len(DILATIONS)
    cur = pl.BlockSpec((None, None, ch, LANES), lambda b, s, c: (b, s, c, 0))
    prv = pl.BlockSpec((None, None, ch, LANES), lambda b, s, c: (b, s, jnp.maximum(c - 1, 0), 0))
    bias_spec = pl.BlockSpec((nb, 2, None, 2 * KEYS_BACK, 2 * KEYS_BACK), lambda b, s, c: (0, 0, s, 0, 0))
    return pl.pallas_call(
        _attn_kernel,
        grid=(batch, N_SLABS, nch),
        in_specs=[cur, prv, cur, prv, cur, bias_spec],
        out_specs=cur,
        out_shape=jax.ShapeDtypeStruct(q.shape, F32),
        scratch_shapes=[pltpu.VMEM((ch, LANES), F32)] * 5 + [pltpu.VMEM((nb, ch, LANES), F32)] * 3,
        compiler_params=pltpu.CompilerParams(dimension_semantics=("parallel", "parallel", "parallel"),
                                             vmem_limit_bytes=VMEM_LIMIT),
        name="dilated_attention",
    )(q, k, k, v, v, bias_tab)


def _rec_kernel(h_ref, g_ref, win_ref, cw_ref, cb_ref, wa_ref, ba_ref, wx_ref, bx_ref, lam_ref, wout_ref,
                o_ref, x_scr, a_scr, b_scr, hl_scr, pl_scr, carry_scr):
    n_seq, tm, width = h_ref.shape
    n_slabs = width // LANES
    seg = tm // SUBLANES + 1

    @pl.when(pl.program_id(1) == 0)
    def _():
        x_scr[:, 0:SUBLANES, :] = jnp.zeros((n_seq, SUBLANES, width), F32)
        carry_scr[...] = jnp.zeros_like(carry_scr)

    neg_lam = -lam_ref[...]
    softplus = jnp.maximum(neg_lam, 0.0) + jnp.log1p(jnp.exp(-jnp.abs(neg_lam)))
    half_c = (0.5 * LRU_C) * softplus
    cw = cw_ref[...]
    blk = width // LRU_BLOCKS
    gates = []
    for u in range(n_seq):
        hn = _rms(h_ref[u], g_ref[...]).astype(BF16)
        z = _dot(hn, win_ref[...])
        gates.append(z[:, width:])
        x_scr[u, SUBLANES:, :] = z[:, :width]
        xb = cb_ref[...] + cw[3:4] * x_scr[u, SUBLANES:, :]
        for k in range(1, cw.shape[0]):
            xb = xb + cw[3 - k:4 - k] * x_scr[u, pl.ds(SUBLANES - k, tm), :]
        x_scr[u, 0:SUBLANES, :] = x_scr[u, tm:, :]

        xb16 = xb.astype(BF16)
        half_ga = jnp.concatenate([_dot(xb16[:, n * blk:(n + 1) * blk], wa_ref[n]) for n in range(LRU_BLOCKS)],
                                  axis=-1) + ba_ref[...]
        half_gx = jnp.concatenate([_dot(xb16[:, n * blk:(n + 1) * blk], wx_ref[n]) for n in range(LRU_BLOCKS)],
                                  axis=-1) + bx_ref[...]
        neg_log_a = half_c * jnp.tanh(half_ga) + half_c
        a = jnp.exp2(neg_log_a * (-LOG2_E))
        one_minus_a2 = jnp.tanh(neg_log_a) * (1.0 + a * a)
        root = jnp.where(one_minus_a2 > 0.0, one_minus_a2 * lax.rsqrt(one_minus_a2), 0.0)
        b = root * ((0.5 * jnp.tanh(half_gx) + 0.5) * xb)
        for c in range(n_slabs):
            a_scr[u, c, 0:tm, :] = a[:, c * LANES:(c + 1) * LANES]
            b_scr[u, c, 0:tm, :] = b[:, c * LANES:(c + 1) * LANES]
            a_scr[u, c, tm:, :] = jnp.ones((SUBLANES * seg - tm, LANES), F32)
            b_scr[u, c, tm:, :] = jnp.zeros((SUBLANES * seg - tm, LANES), F32)

    def scan(u):
        def local_scan(j, carry):
            hl, pr = carry
            step = pl.ds(j, SUBLANES, stride=seg)
            out = pl.ds(pl.multiple_of(j * SUBLANES, SUBLANES), SUBLANES)
            hl_new, pr_new = [], []
            for c in range(n_slabs):
                a_j = a_scr[u, c, step, :]
                hl_c = a_j * hl[c] + b_scr[u, c, step, :]
                pr_c = a_j * pr[c]
                hl_scr[c, out, :] = hl_c
                pl_scr[c, out, :] = pr_c
                hl_new.append(hl_c)
                pr_new.append(pr_c)
            return tuple(hl_new), tuple(pr_new)

        zeros = tuple(jnp.zeros((SUBLANES, LANES), F32) for _ in range(n_slabs))
        ones = tuple(jnp.ones((SUBLANES, LANES), F32) for _ in range(n_slabs))
        hl_end, pr_end = lax.fori_loop(0, seg, local_scan, (zeros, ones), unroll=5)

        starts = []
        for c in range(n_slabs):
            state = carry_scr[u, c, 0:1, :]
            rows = []
            for s in range(SUBLANES):
                rows.append(state)
                state = pr_end[c][s:s + 1, :] * state + hl_end[c][s:s + 1, :]
            carry_scr[u, c, 0:1, :] = state
            starts.append(jnp.concatenate(rows, axis=0))

        def apply_carry(j, carry):
            step = pl.ds(j, SUBLANES, stride=seg)
            src = pl.ds(pl.multiple_of(j * SUBLANES, SUBLANES), SUBLANES)
            for c in range(n_slabs):
                a_scr[u, c, step, :] = hl_scr[c, src, :] + pl_scr[c, src, :] * starts[c]
            return carry

        lax.fori_loop(0, seg, apply_carry, 0, unroll=5)

    for u in range(n_seq):
        scan(u)

    for u in range(n_seq):
        hs = jnp.concatenate([a_scr[u, c, 0:tm, :] for c in range(n_slabs)], axis=-1)
        y = (hs * jax.nn.gelu(gates[u])).astype(BF16)
        o_ref[u] = h_ref[u] + _dot(y, wout_ref[...])


def _recurrent(h, norm, w_in, conv_w, conv_b, wa, ba, wx, bx, lam, w_out, *, batch, seq):
    t, d = h.shape
    tm = TOKEN_TILE
    n_seq = REC_LANES
    tok = pl.BlockSpec((n_seq, tm, d), lambda b, j: (b, j, 0))
    consts = [norm, w_in, conv_w, conv_b, wa, ba, wx, bx, lam, w_out]
    n_slabs = d // LANES
    scan_rows = SUBLANES * (tm // SUBLANES + 1)
    out = pl.pallas_call(
        _rec_kernel,
        grid=(batch // n_seq, seq // tm),
        in_specs=[tok] + [_resident(c.shape) for c in consts],
        out_specs=tok,
        out_shape=jax.ShapeDtypeStruct((batch, seq, d), F32),
        scratch_shapes=[pltpu.VMEM((n_seq, tm + SUBLANES, d), F32),
                        pltpu.VMEM((n_seq, n_slabs, scan_rows, LANES), F32),
                        pltpu.VMEM((n_seq, n_slabs, scan_rows, LANES), F32),
                        pltpu.VMEM((n_slabs, scan_rows, LANES), F32), pltpu.VMEM((n_slabs, scan_rows, LANES), F32),
                        pltpu.VMEM((n_seq, n_slabs, SUBLANES, LANES), F32)],
        compiler_params=pltpu.CompilerParams(dimension_semantics=("parallel", "arbitrary"),
                                             vmem_limit_bytes=VMEM_LIMIT),
        name="recurrent",
    )(h.reshape(batch, seq, d), *consts)
    return out.reshape(t, d)


def kernel(x, p, rel_bias, ffn1_norm, ffn1_w_gate, ffn1_w_up, ffn1_w_down, mix_norm, hyb_w_in, hyb_conv_w, hyb_q_gain, hyb_k_gain, hyb_w_out, rec_w_in, rec_conv_w, rec_conv_b, lru_wa, lru_ba, lru_wx, lru_bx, lru_lambda, rec_w_out, ffn2_norm, ffn2_w_gate, ffn2_w_up, ffn2_w_down, ple_norm, ple_w_gate, ple_w_proj):
    batch, seq, d = x.shape
    depth = p.shape[0]
    t = batch * seq
    assert seq % ATTN_CHUNK == 0 and seq % (TOKEN_TILE * FFN_SUBTILES) == 0 and batch % REC_LANES == 0
    row = lambda a: a[None, :].astype(F32)
    w16 = lambda a: a.astype(BF16)

    bias_tab = _bias_table(rel_bias)
    h = x.reshape(t, d)
    rows = lambda a: a[:, None, :]
    ple = (p.reshape(depth, t, -1), rows(ple_norm), ple_w_gate, ple_w_proj)
    for i in range(depth):
        h = _ffn(h, i, rows(ffn1_norm), ffn1_w_gate, ffn1_w_up, ffn1_w_down, seq=seq)
        ffn2 = (i, rows(ffn2_norm), ffn2_w_gate, ffn2_w_up, ffn2_w_down)
        if i % 2 == 0:
            e = i // 2
            yc, q, k, v = _mix_in(h, row(mix_norm[i]), w16(hyb_w_in[e]), hyb_conv_w[e],
                                  hyb_q_gain[e], hyb_k_gain[e], batch=batch, seq=seq)
            ya = _attention(q, k, v, bias_tab)
            h = _ffn(h, *ffn2, mix=(yc, ya, hyb_w_out, e), ple=ple, seq=seq)
        else:
            o = i // 2
            h = _recurrent(h, row(mix_norm[i]), w16(rec_w_in[o]), rec_conv_w[o], row(rec_conv_b[o]),
                           w16(0.5 * lru_wa[o]), row(0.5 * lru_ba[o]), w16(0.5 * lru_wx[o]), row(0.5 * lru_bx[o]),
                           row(lru_lambda[o]), w16(rec_w_out[o]), batch=batch, seq=seq)
            h = _ffn(h, *ffn2, ple=ple, seq=seq)
    return h.reshape(batch, seq, d)
```

```python
import functools
import math

import jax
import jax.numpy as jnp
from jax import lax
from jax.experimental import pallas as pl
from jax.experimental.pallas import tpu as pltpu

F32 = jnp.float32
BF16 = jnp.bfloat16

EPS = 1e-6
HEAD_DIM = 64
N_HEADS = 8
ATTN_WIDTH = N_HEADS * HEAD_DIM
CONV_WIDTH = 512
DILATIONS = (1, 4, 16)
KEYS_BACK = 128
REL_BUCKETS = 32
REL_MAX_DIST = 2048
LRU_BLOCKS = 4
LRU_C = 8.0
LOG2_E = math.log2(math.e)
LANES = 128
SUBLANES = 8
N_SLABS = ATTN_WIDTH // LANES
ATTN_CHUNK = KEYS_BACK * DILATIONS[-1]
MASKED = -1e30
VMEM_LIMIT = 56 * 1024 * 1024

TOKEN_TILE = 512
FFN_WARMUP = 8
FFN_SUBTILES = 2
REC_LANES = 2
MIX_LANES = 2


def _rms(x, gain):
    ms = jnp.mean(x * x, axis=-1, keepdims=True)
    return x * lax.rsqrt(ms + EPS) * gain


def _gelu_tanh(x):
    k0 = math.sqrt(2.0 / math.pi)
    inner = x * (k0 + (k0 * 0.044715) * (x * x))
    return (0.5 * x) * (1.0 + jnp.tanh(inner))


def _dot(a, b):
    return jnp.dot(a, b, preferred_element_type=F32)


def _resident(shape):
    nd = len(shape)
    return pl.BlockSpec(shape, lambda *_: (0,) * nd, pipeline_mode=pl.Buffered(1))


def _ffn_kernel(has_mix, has_ple, *refs):
    it = iter(refs)
    x_ref = next(it)
    if has_mix:
        yc_ref, ya_ref, wo_ref = next(it), next(it), next(it)
    g_ref, wg_ref, wu_ref, wd_ref = next(it), next(it), next(it), next(it)
    if has_ple:
        p_ref, pg_ref, pwg_ref, pwp_ref = next(it), next(it), next(it), next(it)
    o_ref = next(it)
    wg_s, wu_s, wd_s = next(it), next(it), next(it)
    casts = [(wg_ref, wg_s), (wu_ref, wu_s), (wd_ref, wd_s)]
    if has_mix:
        wo_s = next(it)
        casts.append((wo_ref, wo_s))
    if has_ple:
        pwg_s, pwp_s = next(it), next(it)
        casts += [(pwg_ref, pwg_s), (pwp_ref, pwp_s)]
    s = pl.program_id(0)

    @pl.when(s < FFN_WARMUP)
    def _():
        for src, dst in casts:
            rows = src.shape[0]
            dst[pl.ds(pl.multiple_of(s * rows, rows), rows), :] = src[...].astype(BF16)

    @pl.when(s >= FFN_WARMUP)
    def _():
        tm = TOKEN_TILE
        for sub in range(x_ref.shape[0] // tm):
            rows = slice(sub * tm, (sub + 1) * tm)
            x = x_ref[rows, :]
            if has_mix:
                ya = jnp.concatenate([ya_ref[j, rows, :] for j in range(N_SLABS)], axis=-1).astype(BF16)
                ymix = jnp.concatenate([yc_ref[rows, :], ya], axis=-1)
                x = x + _dot(ymix, wo_s[...])
            hn = _rms(x, g_ref[...]).astype(BF16)
            gate = _dot(hn, wg_s[...])
            up = _dot(hn, wu_s[...])
            act = (gate * jax.nn.sigmoid(gate) * up).astype(BF16)
            x = x + 0.5 * _dot(act, wd_s[...])
            if has_ple:
                hn2 = _rms(x, pg_ref[...]).astype(BF16)
                pgate = jax.nn.sigmoid(_dot(hn2, pwg_s[...]))
                x = x + pgate * _dot(p_ref[rows, :].astype(BF16), pwp_s[...])
            o_ref[rows, :] = x


def _ffn(h, layer, norm, w_gate, w_up, w_down, mix=None, ple=None, *, seq):
    t, d = h.shape
    tb = TOKEN_TILE * (1 if mix is not None else FFN_SUBTILES)
    seq_blocks = seq // tb
    warm = FFN_WARMUP
    blk = lambda s: jnp.maximum(s - warm, 0)
    tok = lambda s: (blk(s), 0)

    def weight(w, idx=layer):
        rows = w.shape[1] // warm
        return pl.BlockSpec((None, rows, w.shape[2]), lambda s: (idx, jnp.minimum(s, warm - 1), 0))

    def vec(g):
        return pl.BlockSpec((None, 1, g.shape[2]), lambda s: (layer, 0, 0))

    def resident(w):
        return pltpu.VMEM(w.shape[1:], BF16)

    args, specs = [h], [pl.BlockSpec((tb, d), tok)]
    scratch = [resident(w_gate), resident(w_up), resident(w_down)]
    if mix is not None:
        yc, ya, w_out, mix_layer = mix
        args += [yc, ya, w_out]
        specs += [pl.BlockSpec((tb, yc.shape[1]), tok),
                  pl.BlockSpec((None, N_SLABS, tb, LANES),
                               lambda s: (blk(s) // seq_blocks, 0, blk(s) % seq_blocks, 0)),
                  weight(w_out, mix_layer)]
        scratch.append(resident(w_out))
    args += [norm, w_gate, w_up, w_down]
    specs += [vec(norm), weight(w_gate), weight(w_up), weight(w_down)]
    if ple is not None:
        p, pnorm, pw_gate, pw_proj = ple
        args += [p, pnorm, pw_gate, pw_proj]
        specs += [pl.BlockSpec((None, tb, p.shape[2]), lambda s: (layer, blk(s), 0)), vec(pnorm),
                  weight(pw_gate), weight(pw_proj)]
        scratch += [resident(pw_gate), resident(pw_proj)]
    return pl.pallas_call(
        functools.partial(_ffn_kernel, mix is not None, ple is not None),
        grid=(warm + t // tb,),
        in_specs=specs,
        out_specs=pl.BlockSpec((tb, d), tok),
        out_shape=jax.ShapeDtypeStruct((t, d), F32),
        scratch_shapes=scratch,
        compiler_params=pltpu.CompilerParams(dimension_semantics=("arbitrary",), vmem_limit_bytes=VMEM_LIMIT),
        name="ffn" + ("_mix" if mix is not None else "") + ("_ple" if ple is not None else ""),
    )(*args)


def _shift_rows(u, prev, k):
    rolled = pltpu.roll(u, shift=k, axis=0)
    row = lax.broadcasted_iota(jnp.int32, u.shape, 0)
    for j in range(k):
        rolled = jnp.where(row == j, prev[8 - k + j:8 - k + j + 1, :], rolled)
    return rolled


def _head_rms(x, gain):
    head_a = lax.broadcasted_iota(jnp.int32, (1, LANES), 1) < HEAD_DIM
    outs = []
    for j in range(N_SLABS):
        xs = x[:, j * LANES:(j + 1) * LANES]
        x2 = xs * xs
        sum_a = jnp.sum(jnp.where(head_a, x2, 0.0), axis=-1, keepdims=True)
        sum_b = jnp.sum(jnp.where(head_a, 0.0, x2), axis=-1, keepdims=True)
        inv = lax.rsqrt(jnp.where(head_a, sum_a, sum_b) * (1.0 / HEAD_DIM) + EPS)
        outs.append(xs * inv * gain[:, j * LANES:(j + 1) * LANES])
    return outs


def _mix_in_kernel(h_ref, g_ref, win_ref, cw_ref, qg_ref, kg_ref,
                   yc_ref, q_ref, k_ref, v_ref, carry_ref):
    @pl.when(pl.program_id(1) == 0)
    def _():
        carry_ref[...] = jnp.zeros_like(carry_ref)

    cwid = CONV_WIDTH
    cw = cw_ref[...]
    for u in range(h_ref.shape[0]):
        hn = _rms(h_ref[u], g_ref[...]).astype(BF16)
        z = _dot(hn, win_ref[...])
        g_b, g_c, c_x = z[:, 0:cwid], z[:, cwid:2 * cwid], z[:, 2 * cwid:3 * cwid]
        x = g_c * c_x
        prev = carry_ref[u]
        conv = cw[0:1] * _shift_rows(x, prev, 2) + cw[1:2] * _shift_rows(x, prev, 1) + cw[2:3] * x
        yc_ref[u] = (g_b * conv).astype(BF16)
        carry_ref[u] = x[x.shape[0] - 8:, :]

        q = z[:, 3 * cwid:3 * cwid + ATTN_WIDTH]
        k = z[:, 3 * cwid + ATTN_WIDTH:3 * cwid + 2 * ATTN_WIDTH]
        v = z[:, 3 * cwid + 2 * ATTN_WIDTH:]
        qn = _head_rms(q, qg_ref[...] * (HEAD_DIM ** -0.5 * LOG2_E))
        kn = _head_rms(k, kg_ref[...])
        for j in range(N_SLABS):
            q_ref[u, j] = qn[j]
            k_ref[u, j] = kn[j]
            v_ref[u, j] = v[:, j * LANES:(j + 1) * LANES]


def _mix_in(h, norm, w_in, conv_w, q_gain, k_gain, *, batch, seq):
    t, d = h.shape
    tm = TOKEN_TILE
    n_seq = MIX_LANES
    tok = lambda b, j: (b, j, 0)
    slab = pl.BlockSpec((n_seq, N_SLABS, tm, LANES), lambda b, j: (b, 0, j, 0))
    qg = jnp.tile(q_gain, N_HEADS)[None, :]
    kg = jnp.tile(k_gain, N_HEADS)[None, :]
    slab_shape = jax.ShapeDtypeStruct((batch, N_SLABS, seq, LANES), F32)
    yc, q, k, v = pl.pallas_call(
        _mix_in_kernel,
        grid=(batch // n_seq, seq // tm),
        in_specs=[pl.BlockSpec((n_seq, tm, d), tok), _resident(norm.shape), _resident(w_in.shape),
                  _resident(conv_w.shape), _resident(qg.shape), _resident(kg.shape)],
        out_specs=[pl.BlockSpec((n_seq, tm, CONV_WIDTH), tok), slab, slab, slab],
        out_shape=[jax.ShapeDtypeStruct((batch, seq, CONV_WIDTH), BF16), slab_shape, slab_shape, slab_shape],
        scratch_shapes=[pltpu.VMEM((n_seq, 8, CONV_WIDTH), F32)],
        compiler_params=pltpu.CompilerParams(dimension_semantics=("parallel", "arbitrary"),
                                             vmem_limit_bytes=VMEM_LIMIT),
        name="mix_in",
    )(h.reshape(batch, seq, d), norm, w_in, conv_w, qg, kg)
    return yc.reshape(t, CONV_WIDTH), q, k, v


def _rel_bucket(dist):
    max_exact = REL_BUCKETS // 2
    n = jnp.maximum(dist, 1).astype(F32)
    large = max_exact + (jnp.log(n / max_exact) / math.log(REL_MAX_DIST / max_exact)
                         * (REL_BUCKETS - max_exact)).astype(jnp.int32)
    large = jnp.minimum(large, REL_BUCKETS - 1)
    return jnp.where(dist < max_exact, dist, large)


def _bias_table(rel_bias):
    qi = jnp.arange(KEYS_BACK)[:, None]
    kj = jnp.arange(2 * KEYS_BACK)[None, :]
    dist = qi + KEYS_BACK - kj
    valid = (dist >= 0) & (dist <= KEYS_BACK)
    tabs = []
    for d in DILATIONS:
        bucket = _rel_bucket(jnp.clip(dist, 0, KEYS_BACK) * d)
        onehot = (bucket[..., None] == jnp.arange(REL_BUCKETS)).astype(F32)
        bias = jnp.einsum("qkb,bh->hqk", onehot, rel_bias.astype(F32) * LOG2_E,
                          precision=lax.Precision.HIGHEST)
        tabs.append(jnp.stack([jnp.where(valid[None], bias, MASKED),
                               jnp.where((valid & (kj >= KEYS_BACK))[None], bias, MASKED)]))
    tab = jnp.stack(tabs)
    return tab.reshape(len(DILATIONS), 2, N_SLABS, 2 * KEYS_BACK, 2 * KEYS_BACK)


def _attn_kernel(q_ref, kp_ref, kc_ref, vp_ref, vc_ref, bias_ref, o_ref,
                 q4_scr, kp4_scr, kc4_scr, vp4_scr, vc4_scr, acc_scr, m_scr, l_scr):
    ch = ATTN_CHUNK
    nq = KEYS_BACK
    quarter = ch // 4
    first_chunk = jnp.where(pl.program_id(2) == 0, 1, 0)
    head_a = lax.broadcasted_iota(jnp.int32, (1, LANES), 1) < HEAD_DIM
    ones = jnp.ones((2 * nq, LANES), BF16)

    for src, dst in ((q_ref, q4_scr), (kp_ref, kp4_scr), (kc_ref, kc4_scr), (vp_ref, vp4_scr), (vc_ref, vc4_scr)):
        for r in range(4):
            dst[r * quarter:(r + 1) * quarter, :] = src[pl.ds(r, quarter, stride=4), :]

    def attend(g, q, k, v, bias, out_rows):
        q2 = jnp.concatenate([jnp.where(head_a, q, 0.0), jnp.where(head_a, 0.0, q)], axis=0).astype(BF16)
        v_aug = jnp.concatenate([v.astype(BF16), ones], axis=-1)
        s = lax.dot_general(q2, k.astype(BF16), (((1,), (1,)), ((), ())), preferred_element_type=F32)
        s = s + bias
        m = jnp.max(s, axis=-1, keepdims=True)
        p = jnp.exp2(s - m).astype(BF16)
        pv = _dot(p, v_aug)
        acc_scr[g, out_rows, :] = jnp.where(head_a, pv[:nq, :LANES], pv[nq:, :LANES])
        l_scr[g, out_rows, :] = jnp.where(head_a, pv[:nq, LANES:], pv[nq:, LANES:])
        m_scr[g, out_rows, :] = jnp.where(head_a, m[:nq], m[nq:])

    def cat(lo, hi):
        return jnp.concatenate([lo, hi], axis=0)

    def unit(g, t):
        first = bias_ref[g, first_chunk]
        if g == 0:
            q_rows = pl.ds(t * nq, nq)
            if t == 0:
                lo, hi = pl.ds(ch - nq, nq), pl.ds(0, nq)
                attend(g, q_ref[q_rows, :], cat(kp_ref[lo, :], kc_ref[hi, :]), cat(vp_ref[lo, :], vc_ref[hi, :]),
                       first, q_rows)
            else:
                kv = pl.ds((t - 1) * nq, 2 * nq)
                attend(g, q_ref[q_rows, :], kc_ref[kv, :], vc_ref[kv, :], bias_ref[g, 0], q_rows)
        elif g == 1:
            r, i = t // 4, t % 4
            q_rows = pl.ds(r * quarter + i * nq, nq)
            if i == 0:
                lo, hi = pl.ds((r + 1) * quarter - nq, nq), pl.ds(r * quarter, nq)
                attend(g, q4_scr[q_rows, :], cat(kp4_scr[lo, :], kc4_scr[hi, :]),
                       cat(vp4_scr[lo, :], vc4_scr[hi, :]), first, q_rows)
            else:
                kv = pl.ds(r * quarter + (i - 1) * nq, 2 * nq)
                attend(g, q4_scr[q_rows, :], kc4_scr[kv, :], vc4_scr[kv, :], bias_ref[g, 0], q_rows)
        else:
            rows = pl.ds((t % 4) * quarter + t // 4, nq, stride=4)
            attend(g, q4_scr[rows, :], cat(kp4_scr[rows, :], kc4_scr[rows, :]),
                   cat(vp4_scr[rows, :], vc4_scr[rows, :]), first, rows)

    for t in range(ch // nq):
        for g in range(len(DILATIONS)):
            unit(g, t)

    def merge(t, carry):
        r, blk = t // 4, t % 4
        rows4 = pl.ds(pl.multiple_of(t * nq, nq), nq)
        rows0 = pl.ds(r + blk * (4 * nq), nq, stride=4)
        sel = lambda scr: [scr[0, rows0, :], scr[1, rows4, :], scr[2, rows4, :]]
        ms = sel(m_scr)
        m_all = functools.reduce(jnp.maximum, ms)
        ws = [jnp.exp2(m - m_all) for m in ms]
        num = sum(w * a for w, a in zip(ws, sel(acc_scr)))
        den = sum(w * l for w, l in zip(ws, sel(l_scr)))
        o_ref[rows0, :] = num / den
        return carry

    lax.fori_loop(0, ch // nq, merge, 0, unroll=2)


def _attention(q, k, v, bias_tab):
    batch, _, seq, _ = q.shape
    ch = ATTN_CHUNK
    nch = seq // ch
    nb = len(DILATIONS)
    cur = pl.BlockSpec((None, None, ch, LANES), lambda b, s, c: (b, s, c, 0))
    prv = pl.BlockSpec((None, None, ch, LANES), lambda b, s, c: (b, s, jnp.maximum(c - 1, 0), 0))
    bias_spec = pl.BlockSpec((nb, 2, None, 2 * KEYS_BACK, 2 * KEYS_BACK), lambda b, s, c: (0, 0, s, 0, 0))
    return pl.pallas_call(
        _attn_kernel,
        grid=(batch, N_SLABS, nch),
        in_specs=[cur, prv, cur, prv, cur, bias_spec],
        out_specs=cur,
        out_shape=jax.ShapeDtypeStruct(q.shape, F32),
        scratch_shapes=[pltpu.VMEM((ch, LANES), F32)] * 5 + [pltpu.VMEM((nb, ch, LANES), F32)] * 3,
        compiler_params=pltpu.CompilerParams(dimension_semantics=("parallel", "parallel", "parallel"),
                                             vmem_limit_bytes=VMEM_LIMIT),
        name="dilated_attention",
    )(q, k, k, v, v, bias_tab)


def _rec_kernel(h_ref, g_ref, win_ref, cw_ref, cb_ref, wa_ref, ba_ref, wx_ref, bx_ref, lam_ref, wout_ref,
                o_ref, x_scr, a_scr, b_scr, hl_scr, pl_scr, carry_scr):
    n_seq, tm, width = h_ref.shape
    n_slabs = width // LANES
    seg = tm // SUBLANES + 1

    @pl.when(pl.program_id(1) == 0)
    def _():
        x_scr[:, 0:SUBLANES, :] = jnp.zeros((n_seq, SUBLANES, width), F32)
        carry_scr[...] = jnp.zeros_like(carry_scr)

    neg_lam = -lam_ref[...]
    softplus = jnp.maximum(neg_lam, 0.0) + jnp.log1p(jnp.exp(-jnp.abs(neg_lam)))
    half_c = (0.5 * LRU_C) * softplus
    cw = cw_ref[...]
    blk = width // LRU_BLOCKS
    gates = []
    for u in range(n_seq):
        hn = _rms(h_ref[u], g_ref[...]).astype(BF16)
        z = _dot(hn, win_ref[...])
        gates.append(z[:, width:])
        x_scr[u, SUBLANES:, :] = z[:, :width]
        xb = cb_ref[...] + cw[3:4] * x_scr[u, SUBLANES:, :]
        for k in range(1, cw.shape[0]):
            xb = xb + cw[3 - k:4 - k] * x_scr[u, pl.ds(SUBLANES - k, tm), :]
        x_scr[u, 0:SUBLANES, :] = x_scr[u, tm:, :]

        xb16 = xb.astype(BF16)
        half_ga = jnp.concatenate([_dot(xb16[:, n * blk:(n + 1) * blk], wa_ref[n]) for n in range(LRU_BLOCKS)],
                                  axis=-1) + ba_ref[...]
        half_gx = jnp.concatenate([_dot(xb16[:, n * blk:(n + 1) * blk], wx_ref[n]) for n in range(LRU_BLOCKS)],
                                  axis=-1) + bx_ref[...]
        neg_log_a = half_c * jnp.tanh(half_ga) + half_c
        a = jnp.exp2(neg_log_a * (-LOG2_E))
        one_minus_a2 = jnp.tanh(neg_log_a) * (1.0 + a * a)
        root = jnp.where(one_minus_a2 > 0.0, one_minus_a2 * lax.rsqrt(one_minus_a2), 0.0)
        b = root * ((0.5 * jnp.tanh(half_gx) + 0.5) * xb)
        for c in range(n_slabs):
            a_scr[u, c, 0:tm, :] = a[:, c * LANES:(c + 1) * LANES]
            b_scr[u, c, 0:tm, :] = b[:, c * LANES:(c + 1) * LANES]
            a_scr[u, c, tm:, :] = jnp.ones((SUBLANES * seg - tm, LANES), F32)
            b_scr[u, c, tm:, :] = jnp.zeros((SUBLANES * seg - tm, LANES), F32)

    def scan(u):
        def local_scan(j, carry):
            hl, pr = carry
            step = pl.ds(j, SUBLANES, stride=seg)
            out = pl.ds(pl.multiple_of(j * SUBLANES, SUBLANES), SUBLANES)
            hl_new, pr_new = [], []
            for c in range(n_slabs):
                a_j = a_scr[u, c, step, :]
                hl_c = a_j * hl[c] + b_scr[u, c, step, :]
                pr_c = a_j * pr[c]
                hl_scr[c, out, :] = hl_c
                pl_scr[c, out, :] = pr_c
                hl_new.append(hl_c)
                pr_new.append(pr_c)
            return tuple(hl_new), tuple(pr_new)

        zeros = tuple(jnp.zeros((SUBLANES, LANES), F32) for _ in range(n_slabs))
        ones = tuple(jnp.ones((SUBLANES, LANES), F32) for _ in range(n_slabs))
        hl_end, pr_end = lax.fori_loop(0, seg, local_scan, (zeros, ones), unroll=5)

        starts = []
        for c in range(n_slabs):
            state = carry_scr[u, c, 0:1, :]
            rows = []
            for s in range(SUBLANES):
                rows.append(state)
                state = pr_end[c][s:s + 1, :] * state + hl_end[c][s:s + 1, :]
            carry_scr[u, c, 0:1, :] = state
            starts.append(jnp.concatenate(rows, axis=0))

        def apply_carry(j, carry):
            step = pl.ds(j, SUBLANES, stride=seg)
            src = pl.ds(pl.multiple_of(j * SUBLANES, SUBLANES), SUBLANES)
            for c in range(n_slabs):
                a_scr[u, c, step, :] = hl_scr[c, src, :] + pl_scr[c, src, :] * starts[c]
            return carry

        lax.fori_loop(0, seg, apply_carry, 0, unroll=5)

    for u in range(n_seq):
        scan(u)

    for u in range(n_seq):
        hs = jnp.concatenate([a_scr[u, c, 0:tm, :] for c in range(n_slabs)], axis=-1)
        y = (hs * _gelu_tanh(gates[u])).astype(BF16)
        o_ref[u] = h_ref[u] + _dot(y, wout_ref[...])


def _recurrent(h, norm, w_in, conv_w, conv_b, wa, ba, wx, bx, lam, w_out, *, batch, seq):
    t, d = h.shape
    tm = TOKEN_TILE
    n_seq = REC_LANES
    tok = pl.BlockSpec((n_seq, tm, d), lambda b, j: (b, j, 0))
    consts = [norm, w_in, conv_w, conv_b, wa, ba, wx, bx, lam, w_out]
    n_slabs = d // LANES
    scan_rows = SUBLANES * (tm // SUBLANES + 1)
    out = pl.pallas_call(
        _rec_kernel,
        grid=(batch // n_seq, seq // tm),
        in_specs=[tok] + [_resident(c.shape) for c in consts],
        out_specs=tok,
        out_shape=jax.ShapeDtypeStruct((batch, seq, d), F32),
        scratch_shapes=[pltpu.VMEM((n_seq, tm + SUBLANES, d), F32),
                        pltpu.VMEM((n_seq, n_slabs, scan_rows, LANES), F32),
                        pltpu.VMEM((n_seq, n_slabs, scan_rows, LANES), F32),
                        pltpu.VMEM((n_slabs, scan_rows, LANES), F32), pltpu.VMEM((n_slabs, scan_rows, LANES), F32),
                        pltpu.VMEM((n_seq, n_slabs, SUBLANES, LANES), F32)],
        compiler_params=pltpu.CompilerParams(dimension_semantics=("parallel", "arbitrary"),
                                             vmem_limit_bytes=VMEM_LIMIT),
        name="recurrent",
    )(h.reshape(batch, seq, d), *consts)
    return out.reshape(t, d)


def kernel(x, p, rel_bias, ffn1_norm, ffn1_w_gate, ffn1_w_up, ffn1_w_down, mix_norm, hyb_w_in, hyb_conv_w, hyb_q_gain, hyb_k_gain, hyb_w_out, rec_w_in, rec_conv_w, rec_conv_b, lru_wa, lru_ba, lru_wx, lru_bx, lru_lambda, rec_w_out, ffn2_norm, ffn2_w_gate, ffn2_w_up, ffn2_w_down, ple_norm, ple_w_gate, ple_w_proj):
    batch, seq, d = x.shape
    depth = p.shape[0]
    t = batch * seq
    assert seq % ATTN_CHUNK == 0 and seq % (TOKEN_TILE * FFN_SUBTILES) == 0 and batch % REC_LANES == 0 \
        and batch % MIX_LANES == 0
    row = lambda a: a[None, :].astype(F32)
    w16 = lambda a: a.astype(BF16)

    bias_tab = _bias_table(rel_bias)
    h = x.reshape(t, d)
    rows = lambda a: a[:, None, :]
    ple = (p.reshape(depth, t, -1), rows(ple_norm), ple_w_gate, ple_w_proj)
    for i in range(depth):
        h = _ffn(h, i, rows(ffn1_norm), ffn1_w_gate, ffn1_w_up, ffn1_w_down, seq=seq)
        ffn2 = (i, rows(ffn2_norm), ffn2_w_gate, ffn2_w_up, ffn2_w_down)
        if i % 2 == 0:
            e = i // 2
            yc, q, k, v = _mix_in(h, row(mix_norm[i]), w16(hyb_w_in[e]), hyb_conv_w[e],
                                  hyb_q_gain[e], hyb_k_gain[e], batch=batch, seq=seq)
            ya = _attention(q, k, v, bias_tab)
            h = _ffn(h, *ffn2, mix=(yc, ya, hyb_w_out, e), ple=ple, seq=seq)
        else:
            o = i // 2
            h = _recurrent(h, row(mix_norm[i]), w16(rec_w_in[o]), rec_conv_w[o], row(rec_conv_b[o]),
                           w16(0.5 * lru_wa[o]), row(0.5 * lru_ba[o]), w16(0.5 * lru_wx[o]), row(0.5 * lru_bx[o]),
                           row(lru_lambda[o]), w16(rec_w_out[o]), batch=batch, seq=seq)
            h = _ffn(h, *ffn2, ple=ple, seq=seq)
    return h.reshape(batch, seq, d)
```

```python
import functools
import math

import jax
import jax.numpy as jnp
from jax import lax
from jax.experimental import pallas as pl
from jax.experimental.pallas import tpu as pltpu

F32 = jnp.float32
BF16 = jnp.bfloat16

EPS = 1e-6
HEAD_DIM = 64
N_HEADS = 8
ATTN_WIDTH = N_HEADS * HEAD_DIM
CONV_WIDTH = 512
DILATIONS = (1, 4, 16)
KEYS_BACK = 128
REL_BUCKETS = 32
REL_MAX_DIST = 2048
LRU_BLOCKS = 4
LRU_C = 8.0
LOG2_E = math.log2(math.e)
LANES = 128
SUBLANES = 8
N_SLABS = ATTN_WIDTH // LANES
ATTN_CHUNK = KEYS_BACK * DILATIONS[-1]
MASKED = -1e30
VMEM_LIMIT = 56 * 1024 * 1024

TOKEN_TILE = 512
FFN_WARMUP = 8
FFN_SUBTILES = 2
REC_LANES = 2
MIX_LANES = 2


def _rms(x, gain):
    ms = jnp.mean(x * x, axis=-1, keepdims=True)
    return x * lax.rsqrt(ms + EPS) * gain


def _gelu_tanh(x):
    k0 = math.sqrt(2.0 / math.pi)
    inner = x * (k0 + (k0 * 0.044715) * (x * x))
    return (0.5 * x) * (1.0 + jnp.tanh(inner))


def _dot(a, b):
    return jnp.dot(a, b, preferred_element_type=F32)


def _resident(shape):
    nd = len(shape)
    return pl.BlockSpec(shape, lambda *_: (0,) * nd, pipeline_mode=pl.Buffered(1))


def _ffn_kernel(has_mix, has_ple, *refs):
    it = iter(refs)
    x_ref = next(it)
    if has_mix:
        yc_ref, ya_ref, wo_ref = next(it), next(it), next(it)
    g_ref, wg_ref, wu_ref, wd_ref = next(it), next(it), next(it), next(it)
    if has_ple:
        p_ref, pg_ref, pwg_ref, pwp_ref = next(it), next(it), next(it), next(it)
    o_ref = next(it)
    wg_s, wu_s, wd_s = next(it), next(it), next(it)
    casts = [(wg_ref, wg_s), (wu_ref, wu_s), (wd_ref, wd_s)]
    if has_mix:
        wo_s = next(it)
        casts.append((wo_ref, wo_s))
    if has_ple:
        pwg_s, pwp_s = next(it), next(it)
        casts += [(pwg_ref, pwg_s), (pwp_ref, pwp_s)]
    s = pl.program_id(0)

    @pl.when(s < FFN_WARMUP)
    def _():
        for src, dst in casts:
            rows = src.shape[0]
            dst[pl.ds(pl.multiple_of(s * rows, rows), rows), :] = src[...].astype(BF16)

    @pl.when(s >= FFN_WARMUP)
    def _():
        tm = TOKEN_TILE
        for sub in range(x_ref.shape[0] // tm):
            rows = slice(sub * tm, (sub + 1) * tm)
            x = x_ref[rows, :]
            if has_mix:
                ya = jnp.concatenate([ya_ref[j, rows, :] for j in range(N_SLABS)], axis=-1).astype(BF16)
                ymix = jnp.concatenate([yc_ref[rows, :], ya], axis=-1)
                x = x + _dot(ymix, wo_s[...])
            hn = _rms(x, g_ref[...]).astype(BF16)
            gate = _dot(hn, wg_s[...])
            up = _dot(hn, wu_s[...])
            act = (gate * jax.nn.sigmoid(gate) * up).astype(BF16)
            x = x + 0.5 * _dot(act, wd_s[...])
            if has_ple:
                hn2 = _rms(x, pg_ref[...]).astype(BF16)
                pgate = jax.nn.sigmoid(_dot(hn2, pwg_s[...]))
                x = x + pgate * _dot(p_ref[rows, :].astype(BF16), pwp_s[...])
            o_ref[rows, :] = x


def _ffn(h, layer, norm, w_gate, w_up, w_down, mix=None, ple=None, *, seq):
    t, d = h.shape
    tb = TOKEN_TILE * (1 if mix is not None else FFN_SUBTILES)
    seq_blocks = seq // tb
    warm = FFN_WARMUP
    blk = lambda s: jnp.maximum(s - warm, 0)
    tok = lambda s: (blk(s), 0)

    def weight(w, idx=layer):
        rows = w.shape[1] // warm
        return pl.BlockSpec((None, rows, w.shape[2]), lambda s: (idx, jnp.minimum(s, warm - 1), 0))

    def vec(g):
        return pl.BlockSpec((None, 1, g.shape[2]), lambda s: (layer, 0, 0))

    def resident(w):
        return pltpu.VMEM(w.shape[1:], BF16)

    args, specs = [h], [pl.BlockSpec((tb, d), tok)]
    scratch = [resident(w_gate), resident(w_up), resident(w_down)]
    if mix is not None:
        yc, ya, w_out, mix_layer = mix
        args += [yc, ya, w_out]
        specs += [pl.BlockSpec((tb, yc.shape[1]), tok),
                  pl.BlockSpec((None, N_SLABS, tb, LANES),
                               lambda s: (blk(s) // seq_blocks, 0, blk(s) % seq_blocks, 0)),
                  weight(w_out, mix_layer)]
        scratch.append(resident(w_out))
    args += [norm, w_gate, w_up, w_down]
    specs += [vec(norm), weight(w_gate), weight(w_up), weight(w_down)]
    if ple is not None:
        p, pnorm, pw_gate, pw_proj = ple
        args += [p, pnorm, pw_gate, pw_proj]
        specs += [pl.BlockSpec((None, tb, p.shape[2]), lambda s: (layer, blk(s), 0)), vec(pnorm),
                  weight(pw_gate), weight(pw_proj)]
        scratch += [resident(pw_gate), resident(pw_proj)]
    return pl.pallas_call(
        functools.partial(_ffn_kernel, mix is not None, ple is not None),
        grid=(warm + t // tb,),
        in_specs=specs,
        out_specs=pl.BlockSpec((tb, d), tok),
        out_shape=jax.ShapeDtypeStruct((t, d), F32),
        scratch_shapes=scratch,
        compiler_params=pltpu.CompilerParams(dimension_semantics=("arbitrary",), vmem_limit_bytes=VMEM_LIMIT),
        name="ffn" + ("_mix" if mix is not None else "") + ("_ple" if ple is not None else ""),
    )(*args)


def _shift_rows(u, prev, k):
    rolled = pltpu.roll(u, shift=k, axis=0)
    row = lax.broadcasted_iota(jnp.int32, u.shape, 0)
    for j in range(k):
        rolled = jnp.where(row == j, prev[SUBLANES - k + j:SUBLANES - k + j + 1, :], rolled)
    return rolled


def _head_rms(x, gain):
    head_a = lax.broadcasted_iota(jnp.int32, (1, LANES), 1) < HEAD_DIM
    outs = []
    for j in range(N_SLABS):
        xs = x[:, j * LANES:(j + 1) * LANES]
        x2 = xs * xs
        sum_a = jnp.sum(jnp.where(head_a, x2, 0.0), axis=-1, keepdims=True)
        sum_b = jnp.sum(jnp.where(head_a, 0.0, x2), axis=-1, keepdims=True)
        inv = lax.rsqrt(jnp.where(head_a, sum_a, sum_b) * (1.0 / HEAD_DIM) + EPS)
        outs.append(xs * inv * gain[:, j * LANES:(j + 1) * LANES])
    return outs


def _mix_in_kernel(h_ref, g_ref, win_ref, cw_ref, qg_ref, kg_ref,
                   yc_ref, q_ref, k_ref, v_ref, carry_ref):
    @pl.when(pl.program_id(1) == 0)
    def _():
        carry_ref[...] = jnp.zeros_like(carry_ref)

    cwid = CONV_WIDTH
    cw = cw_ref[...]
    for u in range(h_ref.shape[0]):
        hn = _rms(h_ref[u], g_ref[...]).astype(BF16)
        z = _dot(hn, win_ref[...])
        g_b, g_c, c_x = z[:, 0:cwid], z[:, cwid:2 * cwid], z[:, 2 * cwid:3 * cwid]
        x = g_c * c_x
        prev = carry_ref[u]
        conv = cw[0:1] * _shift_rows(x, prev, 2) + cw[1:2] * _shift_rows(x, prev, 1) + cw[2:3] * x
        yc_ref[u] = (g_b * conv).astype(BF16)
        carry_ref[u] = x[x.shape[0] - SUBLANES:, :]

        q = z[:, 3 * cwid:3 * cwid + ATTN_WIDTH]
        k = z[:, 3 * cwid + ATTN_WIDTH:3 * cwid + 2 * ATTN_WIDTH]
        v = z[:, 3 * cwid + 2 * ATTN_WIDTH:]
        qn = _head_rms(q, qg_ref[...] * (HEAD_DIM ** -0.5 * LOG2_E))
        kn = _head_rms(k, kg_ref[...])
        for j in range(N_SLABS):
            q_ref[u, j] = qn[j]
            k_ref[u, j] = kn[j]
            v_ref[u, j] = v[:, j * LANES:(j + 1) * LANES]


def _mix_in(h, norm, w_in, conv_w, q_gain, k_gain, *, batch, seq):
    t, d = h.shape
    tm = TOKEN_TILE
    n_seq = MIX_LANES
    tok = lambda b, j: (b, j, 0)
    slab = pl.BlockSpec((n_seq, N_SLABS, tm, LANES), lambda b, j: (b, 0, j, 0))
    qg = jnp.tile(q_gain, N_HEADS)[None, :]
    kg = jnp.tile(k_gain, N_HEADS)[None, :]
    slab_shape = jax.ShapeDtypeStruct((batch, N_SLABS, seq, LANES), F32)
    yc, q, k, v = pl.pallas_call(
        _mix_in_kernel,
        grid=(batch // n_seq, seq // tm),
        in_specs=[pl.BlockSpec((n_seq, tm, d), tok), _resident(norm.shape), _resident(w_in.shape),
                  _resident(conv_w.shape), _resident(qg.shape), _resident(kg.shape)],
        out_specs=[pl.BlockSpec((n_seq, tm, CONV_WIDTH), tok), slab, slab, slab],
        out_shape=[jax.ShapeDtypeStruct((batch, seq, CONV_WIDTH), BF16), slab_shape, slab_shape, slab_shape],
        scratch_shapes=[pltpu.VMEM((n_seq, SUBLANES, CONV_WIDTH), F32)],
        compiler_params=pltpu.CompilerParams(dimension_semantics=("parallel", "arbitrary"),
                                             vmem_limit_bytes=VMEM_LIMIT),
        name="mix_in",
    )(h.reshape(batch, seq, d), norm, w_in, conv_w, qg, kg)
    return yc.reshape(t, CONV_WIDTH), q, k, v


def _rel_bucket(dist):
    max_exact = REL_BUCKETS // 2
    n = jnp.maximum(dist, 1).astype(F32)
    large = max_exact + (jnp.log(n / max_exact) / math.log(REL_MAX_DIST / max_exact)
                         * (REL_BUCKETS - max_exact)).astype(jnp.int32)
    large = jnp.minimum(large, REL_BUCKETS - 1)
    return jnp.where(dist < max_exact, dist, large)


def _bias_table(rel_bias):
    qi = jnp.arange(KEYS_BACK)[:, None]
    kj = jnp.arange(2 * KEYS_BACK)[None, :]
    dist = qi + KEYS_BACK - kj
    valid = (dist >= 0) & (dist <= KEYS_BACK)
    tabs = []
    for d in DILATIONS:
        bucket = _rel_bucket(jnp.clip(dist, 0, KEYS_BACK) * d)
        onehot = (bucket[..., None] == jnp.arange(REL_BUCKETS)).astype(F32)
        bias = jnp.einsum("qkb,bh->hqk", onehot, rel_bias.astype(F32) * LOG2_E,
                          precision=lax.Precision.HIGHEST)
        tabs.append(jnp.stack([jnp.where(valid[None], bias, MASKED),
                               jnp.where((valid & (kj >= KEYS_BACK))[None], bias, MASKED)]))
    tab = jnp.stack(tabs)
    return tab.reshape(len(DILATIONS), 2, N_SLABS, 2 * KEYS_BACK, 2 * KEYS_BACK)


def _attn_kernel(q_ref, kp_ref, kc_ref, vp_ref, vc_ref, bias_ref, o_ref,
                 q4_scr, kp4_scr, kc4_scr, vp4_scr, vc4_scr, acc_scr, m_scr, l_scr):
    ch = ATTN_CHUNK
    nq = KEYS_BACK
    d1 = DILATIONS[1]
    quarter = ch // d1
    first_chunk = jnp.where(pl.program_id(2) == 0, 1, 0)
    head_a = lax.broadcasted_iota(jnp.int32, (1, LANES), 1) < HEAD_DIM
    ones = jnp.ones((2 * nq, LANES), BF16)

    for src, dst in ((q_ref, q4_scr), (kp_ref, kp4_scr), (kc_ref, kc4_scr), (vp_ref, vp4_scr), (vc_ref, vc4_scr)):
        for r in range(d1):
            dst[r * quarter:(r + 1) * quarter, :] = src[pl.ds(r, quarter, stride=d1), :]

    def attend(g, q, k, v, bias, out_rows):
        q2 = jnp.concatenate([jnp.where(head_a, q, 0.0), jnp.where(head_a, 0.0, q)], axis=0).astype(BF16)
        v_aug = jnp.concatenate([v.astype(BF16), ones], axis=-1)
        s = lax.dot_general(q2, k.astype(BF16), (((1,), (1,)), ((), ())), preferred_element_type=F32)
        s = s + bias
        m = jnp.max(s, axis=-1, keepdims=True)
        p = jnp.exp2(s - m).astype(BF16)
        pv = _dot(p, v_aug)
        acc_scr[g, out_rows, :] = jnp.where(head_a, pv[:nq, :LANES], pv[nq:, :LANES])
        l_scr[g, out_rows, :] = jnp.where(head_a, pv[:nq, LANES:], pv[nq:, LANES:])
        m_scr[g, out_rows, :] = jnp.where(head_a, m[:nq], m[nq:])

    def cat(lo, hi):
        return jnp.concatenate([lo, hi], axis=0)

    def unit(g, t):
        first = bias_ref[g, first_chunk]
        if g == 0:
            q_rows = pl.ds(t * nq, nq)
            if t == 0:
                lo, hi = pl.ds(ch - nq, nq), pl.ds(0, nq)
                attend(g, q_ref[q_rows, :], cat(kp_ref[lo, :], kc_ref[hi, :]), cat(vp_ref[lo, :], vc_ref[hi, :]),
                       first, q_rows)
            else:
                kv = pl.ds((t - 1) * nq, 2 * nq)
                attend(g, q_ref[q_rows, :], kc_ref[kv, :], vc_ref[kv, :], bias_ref[g, 0], q_rows)
        elif g == 1:
            r, i = t // d1, t % d1
            q_rows = pl.ds(r * quarter + i * nq, nq)
            if i == 0:
                lo, hi = pl.ds((r + 1) * quarter - nq, nq), pl.ds(r * quarter, nq)
                attend(g, q4_scr[q_rows, :], cat(kp4_scr[lo, :], kc4_scr[hi, :]),
                       cat(vp4_scr[lo, :], vc4_scr[hi, :]), first, q_rows)
            else:
                kv = pl.ds(r * quarter + (i - 1) * nq, 2 * nq)
                attend(g, q4_scr[q_rows, :], kc4_scr[kv, :], vc4_scr[kv, :], bias_ref[g, 0], q_rows)
        else:
            rows = pl.ds((t % d1) * quarter + t // d1, nq, stride=d1)
            attend(g, q4_scr[rows, :], cat(kp4_scr[rows, :], kc4_scr[rows, :]),
                   cat(vp4_scr[rows, :], vc4_scr[rows, :]), first, rows)

    for t in range(ch // nq):
        for g in range(len(DILATIONS)):
            unit(g, t)

    def merge(t, carry):
        r, blk = t // d1, t % d1
        rows4 = pl.ds(pl.multiple_of(t * nq, nq), nq)
        rows0 = pl.ds(r + blk * (d1 * nq), nq, stride=d1)
        sel = lambda scr: [scr[0, rows0, :], scr[1, rows4, :], scr[2, rows4, :]]
        ms = sel(m_scr)
        m_all = functools.reduce(jnp.maximum, ms)
        ws = [jnp.exp2(m - m_all) for m in ms]
        num = sum(w * a for w, a in zip(ws, sel(acc_scr)))
        den = sum(w * l for w, l in zip(ws, sel(l_scr)))
        o_ref[rows0, :] = num / den
        return carry

    lax.fori_loop(0, ch // nq, merge, 0, unroll=2)


def _attention(q, k, v, bias_tab):
    batch, _, seq, _ = q.shape
    ch = ATTN_CHUNK
    nch = seq // ch
    nb = len(DILATIONS)
    cur = pl.BlockSpec((None, None, ch, LANES), lambda b, s, c: (b, s, c, 0))
    prv = pl.BlockSpec((None, None, ch, LANES), lambda b, s, c: (b, s, jnp.maximum(c - 1, 0), 0))
    bias_spec = pl.BlockSpec((nb, 2, None, 2 * KEYS_BACK, 2 * KEYS_BACK), lambda b, s, c: (0, 0, s, 0, 0))
    return pl.pallas_call(
        _attn_kernel,
        grid=(batch, N_SLABS, nch),
        in_specs=[cur, prv, cur, prv, cur, bias_spec],
        out_specs=cur,
        out_shape=jax.ShapeDtypeStruct(q.shape, F32),
        scratch_shapes=[pltpu.VMEM((ch, LANES), F32)] * 5 + [pltpu.VMEM((nb, ch, LANES), F32)] * 3,
        compiler_params=pltpu.CompilerParams(dimension_semantics=("parallel", "parallel", "parallel"),
                                             vmem_limit_bytes=VMEM_LIMIT),
        name="dilated_attention",
    )(q, k, k, v, v, bias_tab)


def _rec_kernel(h_ref, g_ref, win_ref, cw_ref, cb_ref, wa_ref, ba_ref, wx_ref, bx_ref, lam_ref, wout_ref,
                o_ref, x_scr, a_scr, b_scr, hl_scr, pl_scr, carry_scr):
    n_seq, tm, width = h_ref.shape
    n_slabs = width // LANES
    seg = tm // SUBLANES + 1

    @pl.when(pl.program_id(1) == 0)
    def _():
        x_scr[:, 0:SUBLANES, :] = jnp.zeros((n_seq, SUBLANES, width), F32)
        carry_scr[...] = jnp.zeros_like(carry_scr)

    neg_lam = -lam_ref[...]
    softplus = jnp.maximum(neg_lam, 0.0) + jnp.log1p(jnp.exp(-jnp.abs(neg_lam)))
    half_c = (0.5 * LRU_C) * softplus
    cw = cw_ref[...]
    blk = width // LRU_BLOCKS
    gates = []
    for u in range(n_seq):
        hn = _rms(h_ref[u], g_ref[...]).astype(BF16)
        z = _dot(hn, win_ref[...])
        gates.append(z[:, width:])
        x_scr[u, SUBLANES:, :] = z[:, :width]
        xb = cb_ref[...] + cw[3:4] * x_scr[u, SUBLANES:, :]
        for k in range(1, cw.shape[0]):
            xb = xb + cw[3 - k:4 - k] * x_scr[u, pl.ds(SUBLANES - k, tm), :]
        x_scr[u, 0:SUBLANES, :] = x_scr[u, tm:, :]

        xb16 = xb.astype(BF16)
        half_ga = jnp.concatenate([_dot(xb16[:, n * blk:(n + 1) * blk], wa_ref[n]) for n in range(LRU_BLOCKS)],
                                  axis=-1) + ba_ref[...]
        half_gx = jnp.concatenate([_dot(xb16[:, n * blk:(n + 1) * blk], wx_ref[n]) for n in range(LRU_BLOCKS)],
                                  axis=-1) + bx_ref[...]
        neg_log_a = half_c * jnp.tanh(half_ga) + half_c
        a = jnp.exp2(neg_log_a * (-LOG2_E))
        one_minus_a2 = jnp.tanh(neg_log_a) * (1.0 + a * a)
        root = jnp.where(one_minus_a2 > 0.0, one_minus_a2 * lax.rsqrt(one_minus_a2), 0.0)
        b = root * ((0.5 * jnp.tanh(half_gx) + 0.5) * xb)
        for c in range(n_slabs):
            a_scr[u, c, 0:tm, :] = a[:, c * LANES:(c + 1) * LANES]
            b_scr[u, c, 0:tm, :] = b[:, c * LANES:(c + 1) * LANES]
            a_scr[u, c, tm:, :] = jnp.ones((SUBLANES * seg - tm, LANES), F32)
            b_scr[u, c, tm:, :] = jnp.zeros((SUBLANES * seg - tm, LANES), F32)

    def scan(u):
        def local_scan(j, carry):
            hl, pr = carry
            step = pl.ds(j, SUBLANES, stride=seg)
            out = pl.ds(pl.multiple_of(j * SUBLANES, SUBLANES), SUBLANES)
            hl_new, pr_new = [], []
            for c in range(n_slabs):
                a_j = a_scr[u, c, step, :]
                hl_c = a_j * hl[c] + b_scr[u, c, step, :]
                pr_c = a_j * pr[c]
                hl_scr[c, out, :] = hl_c
                pl_scr[c, out, :] = pr_c
                hl_new.append(hl_c)
                pr_new.append(pr_c)
            return tuple(hl_new), tuple(pr_new)

        zeros = tuple(jnp.zeros((SUBLANES, LANES), F32) for _ in range(n_slabs))
        ones = tuple(jnp.ones((SUBLANES, LANES), F32) for _ in range(n_slabs))
        hl_end, pr_end = lax.fori_loop(0, seg, local_scan, (zeros, ones), unroll=5)

        starts = []
        for c in range(n_slabs):
            state = carry_scr[u, c, 0:1, :]
            rows = []
            for s in range(SUBLANES):
                rows.append(state)
                state = pr_end[c][s:s + 1, :] * state + hl_end[c][s:s + 1, :]
            carry_scr[u, c, 0:1, :] = state
            starts.append(jnp.concatenate(rows, axis=0))

        def apply_carry(j, carry):
            step = pl.ds(j, SUBLANES, stride=seg)
            src = pl.ds(pl.multiple_of(j * SUBLANES, SUBLANES), SUBLANES)
            for c in range(n_slabs):
                a_scr[u, c, step, :] = hl_scr[c, src, :] + pl_scr[c, src, :] * starts[c]
            return carry

        lax.fori_loop(0, seg, apply_carry, 0, unroll=5)

    for u in range(n_seq):
        scan(u)

    for u in range(n_seq):
        hs = jnp.concatenate([a_scr[u, c, 0:tm, :] for c in range(n_slabs)], axis=-1)
        y = (hs * _gelu_tanh(gates[u])).astype(BF16)
        o_ref[u] = h_ref[u] + _dot(y, wout_ref[...])


def _recurrent(h, norm, w_in, conv_w, conv_b, wa, ba, wx, bx, lam, w_out, *, batch, seq):
    t, d = h.shape
    tm = TOKEN_TILE
    n_seq = REC_LANES
    tok = pl.BlockSpec((n_seq, tm, d), lambda b, j: (b, j, 0))
    consts = [norm, w_in, conv_w, conv_b, wa, ba, wx, bx, lam, w_out]
    n_slabs = d // LANES
    scan_rows = SUBLANES * (tm // SUBLANES + 1)
    out = pl.pallas_call(
        _rec_kernel,
        grid=(batch // n_seq, seq // tm),
        in_specs=[tok] + [_resident(c.shape) for c in consts],
        out_specs=tok,
        out_shape=jax.ShapeDtypeStruct((batch, seq, d), F32),
        scratch_shapes=[pltpu.VMEM((n_seq, tm + SUBLANES, d), F32),
                        pltpu.VMEM((n_seq, n_slabs, scan_rows, LANES), F32),
                        pltpu.VMEM((n_seq, n_slabs, scan_rows, LANES), F32),
                        pltpu.VMEM((n_slabs, scan_rows, LANES), F32), pltpu.VMEM((n_slabs, scan_rows, LANES), F32),
                        pltpu.VMEM((n_seq, n_slabs, SUBLANES, LANES), F32)],
        compiler_params=pltpu.CompilerParams(dimension_semantics=("parallel", "arbitrary"),
                                             vmem_limit_bytes=VMEM_LIMIT),
        name="recurrent",
    )(h.reshape(batch, seq, d), *consts)
    return out.reshape(t, d)


def kernel(x, p, rel_bias, ffn1_norm, ffn1_w_gate, ffn1_w_up, ffn1_w_down, mix_norm, hyb_w_in, hyb_conv_w, hyb_q_gain, hyb_k_gain, hyb_w_out, rec_w_in, rec_conv_w, rec_conv_b, lru_wa, lru_ba, lru_wx, lru_bx, lru_lambda, rec_w_out, ffn2_norm, ffn2_w_gate, ffn2_w_up, ffn2_w_down, ple_norm, ple_w_gate, ple_w_proj):
    batch, seq, d = x.shape
    depth = p.shape[0]
    t = batch * seq
    assert seq % ATTN_CHUNK == 0 and seq % (TOKEN_TILE * FFN_SUBTILES) == 0 and batch % REC_LANES == 0 \
        and batch % MIX_LANES == 0 and DILATIONS == (1, DILATIONS[1], DILATIONS[1] ** 2)
    row = lambda a: a[None, :].astype(F32)
    w16 = lambda a: a.astype(BF16)

    bias_tab = _bias_table(rel_bias)
    h = x.reshape(t, d)
    rows = lambda a: a[:, None, :]
    ple = (p.reshape(depth, t, -1), rows(ple_norm), ple_w_gate, ple_w_proj)
    for i in range(depth):
        h = _ffn(h, i, rows(ffn1_norm), ffn1_w_gate, ffn1_w_up, ffn1_w_down, seq=seq)
        ffn2 = (i, rows(ffn2_norm), ffn2_w_gate, ffn2_w_up, ffn2_w_down)
        if i % 2 == 0:
            e = i // 2
            yc, q, k, v = _mix_in(h, row(mix_norm[i]), w16(hyb_w_in[e]), hyb_conv_w[e],
                                  hyb_q_gain[e], hyb_k_gain[e], batch=batch, seq=seq)
            ya = _attention(q, k, v, bias_tab)
            h = _ffn(h, *ffn2, mix=(yc, ya, hyb_w_out, e), ple=ple, seq=seq)
        else:
            o = i // 2
            h = _recurrent(h, row(mix_norm[i]), w16(rec_w_in[o]), rec_conv_w[o], row(rec_conv_b[o]),
                           w16(0.5 * lru_wa[o]), row(0.5 * lru_ba[o]), w16(0.5 * lru_wx[o]), row(0.5 * lru_bx[o]),
                           row(lru_lambda[o]), w16(rec_w_out[o]), batch=batch, seq=seq)
            h = _ffn(h, *ffn2, ple=ple, seq=seq)
    return h.reshape(batch, seq, d)
```

```python
import functools
import math

import jax
import jax.numpy as jnp
from jax import lax
from jax.experimental import pallas as pl
from jax.experimental.pallas import tpu as pltpu

F32 = jnp.float32
BF16 = jnp.bfloat16

EPS = 1e-6
HEAD_DIM = 64
N_HEADS = 8
ATTN_WIDTH = N_HEADS * HEAD_DIM
CONV_WIDTH = 512
DILATIONS = (1, 4, 16)
KEYS_BACK = 128
REL_BUCKETS = 32
REL_MAX_DIST = 2048
LRU_BLOCKS = 4
LRU_C = 8.0
LOG2_E = math.log2(math.e)
LANES = 128
SUBLANES = 8
N_SLABS = ATTN_WIDTH // LANES
ATTN_CHUNK = KEYS_BACK * DILATIONS[-1]
MASKED = -1e30
VMEM_LIMIT = 56 * 1024 * 1024

TOKEN_TILE = 512
FFN_WARMUP = 8
FFN_SUBTILES = 2
REC_LANES = 2
MIX_LANES = 2


def _rms(x, gain):
    ms = jnp.mean(x * x, axis=-1, keepdims=True)
    return x * lax.rsqrt(ms + EPS) * gain


def _gelu_tanh(x):
    k0 = math.sqrt(2.0 / math.pi)
    inner = x * (k0 + (k0 * 0.044715) * (x * x))
    return (0.5 * x) * (1.0 + jnp.tanh(inner))


def _dot(a, b):
    return jnp.dot(a, b, preferred_element_type=F32)


def _resident(shape):
    nd = len(shape)
    return pl.BlockSpec(shape, lambda *_: (0,) * nd, pipeline_mode=pl.Buffered(1))


def _ffn_kernel(has_mix, has_ple, *refs):
    it = iter(refs)
    x_ref = next(it)
    if has_mix:
        yc_ref, ya_ref, wo_ref = next(it), next(it), next(it)
    g_ref, wg_ref, wu_ref, wd_ref = next(it), next(it), next(it), next(it)
    if has_ple:
        p_ref, pg_ref, pwg_ref, pwp_ref = next(it), next(it), next(it), next(it)
    o_ref = next(it)
    wg_s, wu_s, wd_s = next(it), next(it), next(it)
    casts = [(wg_ref, wg_s), (wu_ref, wu_s), (wd_ref, wd_s)]
    if has_mix:
        wo_s = next(it)
        casts.append((wo_ref, wo_s))
    if has_ple:
        pwg_s, pwp_s = next(it), next(it)
        casts += [(pwg_ref, pwg_s), (pwp_ref, pwp_s)]
    s = pl.program_id(0)

    @pl.when(s < FFN_WARMUP)
    def _():
        for src, dst in casts:
            rows = src.shape[0]
            dst[pl.ds(pl.multiple_of(s * rows, rows), rows), :] = src[...].astype(BF16)

    @pl.when(s >= FFN_WARMUP)
    def _():
        tm = TOKEN_TILE
        for sub in range(x_ref.shape[0] // tm):
            rows = slice(sub * tm, (sub + 1) * tm)
            x = x_ref[rows, :]
            if has_mix:
                ya = jnp.concatenate([ya_ref[j, rows, :] for j in range(N_SLABS)], axis=-1).astype(BF16)
                ymix = jnp.concatenate([yc_ref[rows, :], ya], axis=-1)
                x = x + _dot(ymix, wo_s[...])
            hn = _rms(x, g_ref[...]).astype(BF16)
            gate = _dot(hn, wg_s[...])
            up = _dot(hn, wu_s[...])
            act = (gate * jax.nn.sigmoid(gate) * up).astype(BF16)
            x = x + 0.5 * _dot(act, wd_s[...])
            if has_ple:
                hn2 = _rms(x, pg_ref[...]).astype(BF16)
                pgate = jax.nn.sigmoid(_dot(hn2, pwg_s[...]))
                x = x + pgate * _dot(p_ref[rows, :].astype(BF16), pwp_s[...])
            o_ref[rows, :] = x


def _ffn(h, layer, norm, w_gate, w_up, w_down, mix=None, ple=None, *, seq):
    t, d = h.shape
    tb = TOKEN_TILE * (1 if mix is not None else FFN_SUBTILES)
    seq_blocks = seq // tb
    warm = FFN_WARMUP
    blk = lambda s: jnp.maximum(s - warm, 0)
    tok = lambda s: (blk(s), 0)

    def weight(w, idx=layer):
        rows = w.shape[1] // warm
        return pl.BlockSpec((None, rows, w.shape[2]), lambda s: (idx, jnp.minimum(s, warm - 1), 0))

    def vec(g):
        return pl.BlockSpec((None, 1, g.shape[2]), lambda s: (layer, 0, 0))

    def resident(w):
        return pltpu.VMEM(w.shape[1:], BF16)

    args, specs = [h], [pl.BlockSpec((tb, d), tok)]
    scratch = [resident(w_gate), resident(w_up), resident(w_down)]
    if mix is not None:
        yc, ya, w_out, mix_layer = mix
        args += [yc, ya, w_out]
        specs += [pl.BlockSpec((tb, yc.shape[1]), tok),
                  pl.BlockSpec((None, N_SLABS, tb, LANES),
                               lambda s: (blk(s) // seq_blocks, 0, blk(s) % seq_blocks, 0)),
                  weight(w_out, mix_layer)]
        scratch.append(resident(w_out))
    args += [norm, w_gate, w_up, w_down]
    specs += [vec(norm), weight(w_gate), weight(w_up), weight(w_down)]
    if ple is not None:
        p, pnorm, pw_gate, pw_proj = ple
        args += [p, pnorm, pw_gate, pw_proj]
        specs += [pl.BlockSpec((None, tb, p.shape[2]), lambda s: (layer, blk(s), 0)), vec(pnorm),
                  weight(pw_gate), weight(pw_proj)]
        scratch += [resident(pw_gate), resident(pw_proj)]
    return pl.pallas_call(
        functools.partial(_ffn_kernel, mix is not None, ple is not None),
        grid=(warm + t // tb,),
        in_specs=specs,
        out_specs=pl.BlockSpec((tb, d), tok),
        out_shape=jax.ShapeDtypeStruct((t, d), F32),
        scratch_shapes=scratch,
        compiler_params=pltpu.CompilerParams(dimension_semantics=("arbitrary",), vmem_limit_bytes=VMEM_LIMIT),
        name="ffn" + ("_mix" if mix is not None else "") + ("_ple" if ple is not None else ""),
    )(*args)


def _shift_rows(u, prev, k):
    rolled = pltpu.roll(u, shift=k, axis=0)
    row = lax.broadcasted_iota(jnp.int32, u.shape, 0)
    for j in range(k):
        rolled = jnp.where(row == j, prev[SUBLANES - k + j:SUBLANES - k + j + 1, :], rolled)
    return rolled


def _head_rms(x, gain):
    head_a = lax.broadcasted_iota(jnp.int32, (1, LANES), 1) < HEAD_DIM
    outs = []
    for j in range(N_SLABS):
        xs = x[:, j * LANES:(j + 1) * LANES]
        x2 = xs * xs
        sum_a = jnp.sum(jnp.where(head_a, x2, 0.0), axis=-1, keepdims=True)
        sum_b = jnp.sum(jnp.where(head_a, 0.0, x2), axis=-1, keepdims=True)
        inv = lax.rsqrt(jnp.where(head_a, sum_a, sum_b) * (1.0 / HEAD_DIM) + EPS)
        outs.append(xs * inv * gain[:, j * LANES:(j + 1) * LANES])
    return outs


def _mix_in_kernel(h_ref, g_ref, win_ref, cw_ref, qg_ref, kg_ref,
                   yc_ref, q_ref, k_ref, v_ref, carry_ref):
    @pl.when(pl.program_id(1) == 0)
    def _():
        carry_ref[...] = jnp.zeros_like(carry_ref)

    cwid = CONV_WIDTH
    cw = cw_ref[...]
    for u in range(h_ref.shape[0]):
        hn = _rms(h_ref[u], g_ref[...]).astype(BF16)
        z = _dot(hn, win_ref[...])
        g_b, g_c, c_x = z[:, 0:cwid], z[:, cwid:2 * cwid], z[:, 2 * cwid:3 * cwid]
        x = g_c * c_x
        prev = carry_ref[u]
        conv = cw[0:1] * _shift_rows(x, prev, 2) + cw[1:2] * _shift_rows(x, prev, 1) + cw[2:3] * x
        yc_ref[u] = (g_b * conv).astype(BF16)
        carry_ref[u] = x[x.shape[0] - SUBLANES:, :]

        q = z[:, 3 * cwid:3 * cwid + ATTN_WIDTH]
        k = z[:, 3 * cwid + ATTN_WIDTH:3 * cwid + 2 * ATTN_WIDTH]
        v = z[:, 3 * cwid + 2 * ATTN_WIDTH:]
        qn = _head_rms(q, qg_ref[...] * (HEAD_DIM ** -0.5 * LOG2_E))
        kn = _head_rms(k, kg_ref[...])
        for j in range(N_SLABS):
            q_ref[u, j] = qn[j]
            k_ref[u, j] = kn[j]
            v_ref[u, j] = v[:, j * LANES:(j + 1) * LANES]


def _mix_in(h, norm, w_in, conv_w, q_gain, k_gain, *, batch, seq):
    t, d = h.shape
    tm = TOKEN_TILE
    n_seq = MIX_LANES
    tok = lambda b, j: (b, j, 0)
    slab = pl.BlockSpec((n_seq, N_SLABS, tm, LANES), lambda b, j: (b, 0, j, 0))
    qg = jnp.tile(q_gain, N_HEADS)[None, :]
    kg = jnp.tile(k_gain, N_HEADS)[None, :]
    slab_shape = jax.ShapeDtypeStruct((batch, N_SLABS, seq, LANES), F32)
    yc, q, k, v = pl.pallas_call(
        _mix_in_kernel,
        grid=(batch // n_seq, seq // tm),
        in_specs=[pl.BlockSpec((n_seq, tm, d), tok), _resident(norm.shape), _resident(w_in.shape),
                  _resident(conv_w.shape), _resident(qg.shape), _resident(kg.shape)],
        out_specs=[pl.BlockSpec((n_seq, tm, CONV_WIDTH), tok), slab, slab, slab],
        out_shape=[jax.ShapeDtypeStruct((batch, seq, CONV_WIDTH), BF16), slab_shape, slab_shape, slab_shape],
        scratch_shapes=[pltpu.VMEM((n_seq, SUBLANES, CONV_WIDTH), F32)],
        compiler_params=pltpu.CompilerParams(dimension_semantics=("parallel", "arbitrary"),
                                             vmem_limit_bytes=VMEM_LIMIT),
        name="mix_in",
    )(h.reshape(batch, seq, d), norm, w_in, conv_w, qg, kg)
    return yc.reshape(t, CONV_WIDTH), q, k, v


def _rel_bucket(dist):
    max_exact = REL_BUCKETS // 2
    n = jnp.maximum(dist, 1).astype(F32)
    large = max_exact + (jnp.log(n / max_exact) / math.log(REL_MAX_DIST / max_exact)
                         * (REL_BUCKETS - max_exact)).astype(jnp.int32)
    large = jnp.minimum(large, REL_BUCKETS - 1)
    return jnp.where(dist < max_exact, dist, large)


def _bias_table(rel_bias):
    qi = jnp.arange(KEYS_BACK)[:, None]
    kj = jnp.arange(2 * KEYS_BACK)[None, :]
    dist = qi + KEYS_BACK - kj
    valid = (dist >= 0) & (dist <= KEYS_BACK)
    tabs = []
    for d in DILATIONS:
        bucket = _rel_bucket(jnp.clip(dist, 0, KEYS_BACK) * d)
        onehot = (bucket[..., None] == jnp.arange(REL_BUCKETS)).astype(F32)
        bias = jnp.einsum("qkb,bh->hqk", onehot, rel_bias.astype(F32) * LOG2_E,
                          precision=lax.Precision.HIGHEST)
        tabs.append(jnp.stack([jnp.where(valid[None], bias, MASKED),
                               jnp.where((valid & (kj >= KEYS_BACK))[None], bias, MASKED)]))
    tab = jnp.stack(tabs)
    return tab.reshape(len(DILATIONS), 2, N_SLABS, 2 * KEYS_BACK, 2 * KEYS_BACK)


def _attn_kernel(q_ref, kp_ref, kc_ref, vp_ref, vc_ref, bias_ref, o_ref,
                 q4_scr, kp4_scr, kc4_scr, vp4_scr, vc4_scr, acc_scr, m_scr, l_scr):
    ch = ATTN_CHUNK
    nq = KEYS_BACK
    d1 = DILATIONS[1]
    quarter = ch // d1
    first_chunk = jnp.where(pl.program_id(2) == 0, 1, 0)
    head_a = lax.broadcasted_iota(jnp.int32, (1, LANES), 1) < HEAD_DIM
    ones = jnp.ones((2 * nq, LANES), BF16)

    for src, dst in ((q_ref, q4_scr), (kp_ref, kp4_scr), (kc_ref, kc4_scr), (vp_ref, vp4_scr), (vc_ref, vc4_scr)):
        for r in range(d1):
            dst[r * quarter:(r + 1) * quarter, :] = src[pl.ds(r, quarter, stride=d1), :]

    def attend(g, q, k, v, bias, out_rows):
        q2 = jnp.concatenate([jnp.where(head_a, q, 0.0), jnp.where(head_a, 0.0, q)], axis=0).astype(BF16)
        v_aug = jnp.concatenate([v.astype(BF16), ones], axis=-1)
        s = lax.dot_general(q2, k.astype(BF16), (((1,), (1,)), ((), ())), preferred_element_type=F32)
        s = s + bias
        m = jnp.max(s, axis=-1, keepdims=True)
        p = jnp.exp2(s - m).astype(BF16)
        pv = _dot(p, v_aug)
        acc_scr[g, out_rows, :] = jnp.where(head_a, pv[:nq, :LANES], pv[nq:, :LANES])
        l_scr[g, out_rows, :] = jnp.where(head_a, pv[:nq, LANES:], pv[nq:, LANES:])
        m_scr[g, out_rows, :] = jnp.where(head_a, m[:nq], m[nq:])

    def cat(lo, hi):
        return jnp.concatenate([lo, hi], axis=0)

    def unit(g, t):
        first = bias_ref[g, first_chunk]
        if g == 0:
            q_rows = pl.ds(t * nq, nq)
            if t == 0:
                lo, hi = pl.ds(ch - nq, nq), pl.ds(0, nq)
                attend(g, q_ref[q_rows, :], cat(kp_ref[lo, :], kc_ref[hi, :]), cat(vp_ref[lo, :], vc_ref[hi, :]),
                       first, q_rows)
            else:
                kv = pl.ds((t - 1) * nq, 2 * nq)
                attend(g, q_ref[q_rows, :], kc_ref[kv, :], vc_ref[kv, :], bias_ref[g, 0], q_rows)
        elif g == 1:
            r, i = t // d1, t % d1
            q_rows = pl.ds(r * quarter + i * nq, nq)
            if i == 0:
                lo, hi = pl.ds((r + 1) * quarter - nq, nq), pl.ds(r * quarter, nq)
                attend(g, q4_scr[q_rows, :], cat(kp4_scr[lo, :], kc4_scr[hi, :]),
                       cat(vp4_scr[lo, :], vc4_scr[hi, :]), first, q_rows)
            else:
                kv = pl.ds(r * quarter + (i - 1) * nq, 2 * nq)
                attend(g, q4_scr[q_rows, :], kc4_scr[kv, :], vc4_scr[kv, :], bias_ref[g, 0], q_rows)
        else:
            rows = pl.ds((t % d1) * quarter + t // d1, nq, stride=d1)
            attend(g, q4_scr[rows, :], cat(kp4_scr[rows, :], kc4_scr[rows, :]),
                   cat(vp4_scr[rows, :], vc4_scr[rows, :]), first, rows)

    def merge(r, blk):
        rows4 = pl.ds(r * quarter + blk * nq, nq)
        rows0 = pl.ds(r + blk * (d1 * nq), nq, stride=d1)
        sel = lambda scr: [scr[0, rows0, :], scr[1, rows4, :], scr[2, rows4, :]]
        ms = sel(m_scr)
        m_all = functools.reduce(jnp.maximum, ms)
        ws = [jnp.exp2(m - m_all) for m in ms]
        num = sum(w * a for w, a in zip(ws, sel(acc_scr)))
        den = sum(w * l for w, l in zip(ws, sel(l_scr)))
        o_ref[rows0, :] = num / den

    n_blocks = ch // nq
    per_res = n_blocks // d1
    for t in range(n_blocks):
        unit(0, t)
        if t % per_res == per_res - 1:
            unit(2, d1 * (t // per_res))
    for r in range(d1):
        for blk in range(per_res):
            unit(1, per_res * r + blk)
            merge(r, blk)
        if r + 1 < d1:
            for r2 in range(per_res):
                unit(2, r + 1 + d1 * r2)


def _attention(q, k, v, bias_tab):
    batch, _, seq, _ = q.shape
    ch = ATTN_CHUNK
    nch = seq // ch
    nb = len(DILATIONS)
    cur = pl.BlockSpec((None, None, ch, LANES), lambda b, s, c: (b, s, c, 0))
    prv = pl.BlockSpec((None, None, ch, LANES), lambda b, s, c: (b, s, jnp.maximum(c - 1, 0), 0))
    bias_spec = pl.BlockSpec((nb, 2, None, 2 * KEYS_BACK, 2 * KEYS_BACK), lambda b, s, c: (0, 0, s, 0, 0))
    return pl.pallas_call(
        _attn_kernel,
        grid=(batch, N_SLABS, nch),
        in_specs=[cur, prv, cur, prv, cur, bias_spec],
        out_specs=cur,
        out_shape=jax.ShapeDtypeStruct(q.shape, F32),
        scratch_shapes=[pltpu.VMEM((ch, LANES), F32)] * 5 + [pltpu.VMEM((nb, ch, LANES), F32)] * 3,
        compiler_params=pltpu.CompilerParams(dimension_semantics=("parallel", "parallel", "parallel"),
                                             vmem_limit_bytes=VMEM_LIMIT),
        name="dilated_attention",
    )(q, k, k, v, v, bias_tab)


def _rec_kernel(h_ref, g_ref, win_ref, cw_ref, cb_ref, wa_ref, ba_ref, wx_ref, bx_ref, lam_ref, wout_ref,
                o_ref, x_scr, a_scr, b_scr, hl_scr, pl_scr, carry_scr):
    n_seq, tm, width = h_ref.shape
    n_slabs = width // LANES
    seg = tm // SUBLANES + 1

    @pl.when(pl.program_id(1) == 0)
    def _():
        x_scr[:, 0:SUBLANES, :] = jnp.zeros((n_seq, SUBLANES, width), F32)
        carry_scr[...] = jnp.zeros_like(carry_scr)

    neg_lam = -lam_ref[...]
    softplus = jnp.maximum(neg_lam, 0.0) + jnp.log1p(jnp.exp(-jnp.abs(neg_lam)))
    half_c = (0.5 * LRU_C) * softplus
    cw = cw_ref[...]
    blk = width // LRU_BLOCKS
    gates = []
    for u in range(n_seq):
        hn = _rms(h_ref[u], g_ref[...]).astype(BF16)
        z = _dot(hn, win_ref[...])
        gates.append(z[:, width:])
        x_scr[u, SUBLANES:, :] = z[:, :width]
        xb = cb_ref[...] + cw[3:4] * x_scr[u, SUBLANES:, :]
        for k in range(1, cw.shape[0]):
            xb = xb + cw[3 - k:4 - k] * x_scr[u, pl.ds(SUBLANES - k, tm), :]
        x_scr[u, 0:SUBLANES, :] = x_scr[u, tm:, :]

        xb16 = xb.astype(BF16)
        half_ga = jnp.concatenate([_dot(xb16[:, n * blk:(n + 1) * blk], wa_ref[n]) for n in range(LRU_BLOCKS)],
                                  axis=-1) + ba_ref[...]
        half_gx = jnp.concatenate([_dot(xb16[:, n * blk:(n + 1) * blk], wx_ref[n]) for n in range(LRU_BLOCKS)],
                                  axis=-1) + bx_ref[...]
        neg_log_a = half_c * jnp.tanh(half_ga) + half_c
        a = jnp.exp2(neg_log_a * (-LOG2_E))
        one_minus_a2 = jnp.tanh(neg_log_a) * (1.0 + a * a)
        root = jnp.where(one_minus_a2 > 0.0, one_minus_a2 * lax.rsqrt(one_minus_a2), 0.0)
        b = root * ((0.5 * jnp.tanh(half_gx) + 0.5) * xb)
        for c in range(n_slabs):
            a_scr[u, c, 0:tm, :] = a[:, c * LANES:(c + 1) * LANES]
            b_scr[u, c, 0:tm, :] = b[:, c * LANES:(c + 1) * LANES]
            a_scr[u, c, tm:, :] = jnp.ones((SUBLANES * seg - tm, LANES), F32)
            b_scr[u, c, tm:, :] = jnp.zeros((SUBLANES * seg - tm, LANES), F32)

    def scan(u):
        def local_scan(j, carry):
            hl, pr = carry
            step = pl.ds(j, SUBLANES, stride=seg)
            out = pl.ds(pl.multiple_of(j * SUBLANES, SUBLANES), SUBLANES)
            hl_new, pr_new = [], []
            for c in range(n_slabs):
                a_j = a_scr[u, c, step, :]
                hl_c = a_j * hl[c] + b_scr[u, c, step, :]
                pr_c = a_j * pr[c]
                hl_scr[c, out, :] = hl_c
                pl_scr[c, out, :] = pr_c
                hl_new.append(hl_c)
                pr_new.append(pr_c)
            return tuple(hl_new), tuple(pr_new)

        zeros = tuple(jnp.zeros((SUBLANES, LANES), F32) for _ in range(n_slabs))
        ones = tuple(jnp.ones((SUBLANES, LANES), F32) for _ in range(n_slabs))
        hl_end, pr_end = lax.fori_loop(0, seg, local_scan, (zeros, ones), unroll=5)

        starts = []
        for c in range(n_slabs):
            state = carry_scr[u, c, 0:1, :]
            rows = []
            for s in range(SUBLANES):
                rows.append(state)
                state = pr_end[c][s:s + 1, :] * state + hl_end[c][s:s + 1, :]
            carry_scr[u, c, 0:1, :] = state
            starts.append(jnp.concatenate(rows, axis=0))

        def apply_carry(j, carry):
            step = pl.ds(j, SUBLANES, stride=seg)
            src = pl.ds(pl.multiple_of(j * SUBLANES, SUBLANES), SUBLANES)
            for c in range(n_slabs):
                a_scr[u, c, step, :] = hl_scr[c, src, :] + pl_scr[c, src, :] * starts[c]
            return carry

        lax.fori_loop(0, seg, apply_carry, 0, unroll=5)

    for u in range(n_seq):
        scan(u)

    for u in range(n_seq):
        hs = jnp.concatenate([a_scr[u, c, 0:tm, :] for c in range(n_slabs)], axis=-1)
        y = (hs * _gelu_tanh(gates[u])).astype(BF16)
        o_ref[u] = h_ref[u] + _dot(y, wout_ref[...])


def _recurrent(h, norm, w_in, conv_w, conv_b, wa, ba, wx, bx, lam, w_out, *, batch, seq):
    t, d = h.shape
    tm = TOKEN_TILE
    n_seq = REC_LANES
    tok = pl.BlockSpec((n_seq, tm, d), lambda b, j: (b, j, 0))
    consts = [norm, w_in, conv_w, conv_b, wa, ba, wx, bx, lam, w_out]
    n_slabs = d // LANES
    scan_rows = SUBLANES * (tm // SUBLANES + 1)
    out = pl.pallas_call(
        _rec_kernel,
        grid=(batch // n_seq, seq // tm),
        in_specs=[tok] + [_resident(c.shape) for c in consts],
        out_specs=tok,
        out_shape=jax.ShapeDtypeStruct((batch, seq, d), F32),
        scratch_shapes=[pltpu.VMEM((n_seq, tm + SUBLANES, d), F32),
                        pltpu.VMEM((n_seq, n_slabs, scan_rows, LANES), F32),
                        pltpu.VMEM((n_seq, n_slabs, scan_rows, LANES), F32),
                        pltpu.VMEM((n_slabs, scan_rows, LANES), F32), pltpu.VMEM((n_slabs, scan_rows, LANES), F32),
                        pltpu.VMEM((n_seq, n_slabs, SUBLANES, LANES), F32)],
        compiler_params=pltpu.CompilerParams(dimension_semantics=("parallel", "arbitrary"),
                                             vmem_limit_bytes=VMEM_LIMIT),
        name="recurrent",
    )(h.reshape(batch, seq, d), *consts)
    return out.reshape(t, d)


def kernel(x, p, rel_bias, ffn1_norm, ffn1_w_gate, ffn1_w_up, ffn1_w_down, mix_norm, hyb_w_in, hyb_conv_w, hyb_q_gain, hyb_k_gain, hyb_w_out, rec_w_in, rec_conv_w, rec_conv_b, lru_wa, lru_ba, lru_wx, lru_bx, lru_lambda, rec_w_out, ffn2_norm, ffn2_w_gate, ffn2_w_up, ffn2_w_down, ple_norm, ple_w_gate, ple_w_proj):
    batch, seq, d = x.shape
    depth = p.shape[0]
    t = batch * seq
    assert seq % ATTN_CHUNK == 0 and seq % (TOKEN_TILE * FFN_SUBTILES) == 0 and batch % REC_LANES == 0 \
        and batch % MIX_LANES == 0 and DILATIONS == (1, DILATIONS[1], DILATIONS[1] ** 2)
    row = lambda a: a[None, :].astype(F32)
    w16 = lambda a: a.astype(BF16)

    bias_tab = _bias_table(rel_bias)
    h = x.reshape(t, d)
    rows = lambda a: a[:, None, :]
    ple = (p.reshape(depth, t, -1), rows(ple_norm), ple_w_gate, ple_w_proj)
    for i in range(depth):
        h = _ffn(h, i, rows(ffn1_norm), ffn1_w_gate, ffn1_w_up, ffn1_w_down, seq=seq)
        ffn2 = (i, rows(ffn2_norm), ffn2_w_gate, ffn2_w_up, ffn2_w_down)
        if i % 2 == 0:
            e = i // 2
            yc, q, k, v = _mix_in(h, row(mix_norm[i]), w16(hyb_w_in[e]), hyb_conv_w[e],
                                  hyb_q_gain[e], hyb_k_gain[e], batch=batch, seq=seq)
            ya = _attention(q, k, v, bias_tab)
            h = _ffn(h, *ffn2, mix=(yc, ya, hyb_w_out, e), ple=ple, seq=seq)
        else:
            o = i // 2
            h = _recurrent(h, row(mix_norm[i]), w16(rec_w_in[o]), rec_conv_w[o], row(rec_conv_b[o]),
                           w16(0.5 * lru_wa[o]), row(0.5 * lru_ba[o]), w16(0.5 * lru_wx[o]), row(0.5 * lru_bx[o]),
                           row(lru_lambda[o]), w16(rec_w_out[o]), batch=batch, seq=seq)
            h = _ffn(h, *ffn2, ple=ple, seq=seq)
    return h.reshape(batch, seq, d)
```

```python
import functools
import math

import jax
import jax.numpy as jnp
from jax import lax
from jax.experimental import pallas as pl
from jax.experimental.pallas import tpu as pltpu

F32 = jnp.float32
BF16 = jnp.bfloat16

EPS = 1e-6
HEAD_DIM = 64
N_HEADS = 8
ATTN_WIDTH = N_HEADS * HEAD_DIM
CONV_WIDTH = 512
DILATIONS = (1, 4, 16)
KEYS_BACK = 128
REL_BUCKETS = 32
REL_MAX_DIST = 2048
LRU_BLOCKS = 4
LRU_C = 8.0
LOG2_E = math.log2(math.e)
LANES = 128
SUBLANES = 8
N_SLABS = ATTN_WIDTH // LANES
ATTN_CHUNK = KEYS_BACK * DILATIONS[-1]
MASKED = -1e30
VMEM_LIMIT = 56 * 1024 * 1024

TOKEN_TILE = 512
FFN_WARMUP = 8
FFN_SUBTILES = 2
REC_LANES = 2
MIX_LANES = 2


def _rms(x, gain):
    ms = jnp.mean(x * x, axis=-1, keepdims=True)
    return x * lax.rsqrt(ms + EPS) * gain


def _gelu_tanh(x):
    k0 = math.sqrt(2.0 / math.pi)
    inner = x * (k0 + (k0 * 0.044715) * (x * x))
    return (0.5 * x) * (1.0 + jnp.tanh(inner))


def _dot(a, b):
    return jnp.dot(a, b, preferred_element_type=F32)


def _resident(shape):
    nd = len(shape)
    return pl.BlockSpec(shape, lambda *_: (0,) * nd, pipeline_mode=pl.Buffered(1))


def _ffn_kernel(has_mix, has_ple, *refs):
    it = iter(refs)
    x_ref = next(it)
    if has_mix:
        yc_ref, ya_ref, wo_ref = next(it), next(it), next(it)
    g_ref, wg_ref, wu_ref, wd_ref = next(it), next(it), next(it), next(it)
    if has_ple:
        p_ref, pg_ref, pwg_ref, pwp_ref = next(it), next(it), next(it), next(it)
    o_ref = next(it)
    wg_s, wu_s, wd_s = next(it), next(it), next(it)
    casts = [(wg_ref, wg_s), (wu_ref, wu_s), (wd_ref, wd_s)]
    if has_mix:
        wo_s = next(it)
        casts.append((wo_ref, wo_s))
    if has_ple:
        pwg_s, pwp_s = next(it), next(it)
        casts += [(pwg_ref, pwg_s), (pwp_ref, pwp_s)]
    s = pl.program_id(0)

    @pl.when(s < FFN_WARMUP)
    def _():
        for src, dst in casts:
            rows = src.shape[0]
            dst[pl.ds(pl.multiple_of(s * rows, rows), rows), :] = src[...].astype(BF16)

    @pl.when(s >= FFN_WARMUP)
    def _():
        tm = TOKEN_TILE
        for sub in range(x_ref.shape[0] // tm):
            rows = slice(sub * tm, (sub + 1) * tm)
            x = x_ref[rows, :]
            if has_mix:
                ya = jnp.concatenate([ya_ref[j, rows, :] for j in range(N_SLABS)], axis=-1).astype(BF16)
                ymix = jnp.concatenate([yc_ref[rows, :], ya], axis=-1)
                x = x + _dot(ymix, wo_s[...])
            hn = _rms(x, g_ref[...]).astype(BF16)
            gate = _dot(hn, wg_s[...])
            up = _dot(hn, wu_s[...])
            act = (gate * jax.nn.sigmoid(gate) * up).astype(BF16)
            x = x + 0.5 * _dot(act, wd_s[...])
            if has_ple:
                hn2 = _rms(x, pg_ref[...]).astype(BF16)
                pgate = jax.nn.sigmoid(_dot(hn2, pwg_s[...]))
                x = x + pgate * _dot(p_ref[rows, :].astype(BF16), pwp_s[...])
            o_ref[rows, :] = x


def _ffn(h, layer, norm, w_gate, w_up, w_down, mix=None, ple=None, *, seq):
    t, d = h.shape
    tb = TOKEN_TILE * (1 if mix is not None else FFN_SUBTILES)
    seq_blocks = seq // tb
    warm = FFN_WARMUP
    blk = lambda s: jnp.maximum(s - warm, 0)
    tok = lambda s: (blk(s), 0)

    def weight(w, idx=layer):
        rows = w.shape[1] // warm
        return pl.BlockSpec((None, rows, w.shape[2]), lambda s: (idx, jnp.minimum(s, warm - 1), 0))

    def vec(g):
        return pl.BlockSpec((None, 1, g.shape[2]), lambda s: (layer, 0, 0))

    def resident(w):
        return pltpu.VMEM(w.shape[1:], BF16)

    args, specs = [h], [pl.BlockSpec((tb, d), tok)]
    scratch = [resident(w_gate), resident(w_up), resident(w_down)]
    if mix is not None:
        yc, ya, w_out, mix_layer = mix
        args += [yc, ya, w_out]
        specs += [pl.BlockSpec((tb, yc.shape[1]), tok),
                  pl.BlockSpec((None, N_SLABS, tb, LANES),
                               lambda s: (blk(s) // seq_blocks, 0, blk(s) % seq_blocks, 0)),
                  weight(w_out, mix_layer)]
        scratch.append(resident(w_out))
    args += [norm, w_gate, w_up, w_down]
    specs += [vec(norm), weight(w_gate), weight(w_up), weight(w_down)]
    if ple is not None:
        p, pnorm, pw_gate, pw_proj = ple
        args += [p, pnorm, pw_gate, pw_proj]
        specs += [pl.BlockSpec((None, tb, p.shape[2]), lambda s: (layer, blk(s), 0)), vec(pnorm),
                  weight(pw_gate), weight(pw_proj)]
        scratch += [resident(pw_gate), resident(pw_proj)]
    return pl.pallas_call(
        functools.partial(_ffn_kernel, mix is not None, ple is not None),
        grid=(warm + t // tb,),
        in_specs=specs,
        out_specs=pl.BlockSpec((tb, d), tok),
        out_shape=jax.ShapeDtypeStruct((t, d), F32),
        scratch_shapes=scratch,
        compiler_params=pltpu.CompilerParams(dimension_semantics=("arbitrary",), vmem_limit_bytes=VMEM_LIMIT),
        name="ffn" + ("_mix" if mix is not None else "") + ("_ple" if ple is not None else ""),
    )(*args)


def _shift_rows(u, prev, k):
    rolled = pltpu.roll(u, shift=k, axis=0)
    row = lax.broadcasted_iota(jnp.int32, u.shape, 0)
    for j in range(k):
        rolled = jnp.where(row == j, prev[SUBLANES - k + j:SUBLANES - k + j + 1, :], rolled)
    return rolled


def _head_rms(x, gain):
    head_a = lax.broadcasted_iota(jnp.int32, (1, LANES), 1) < HEAD_DIM
    outs = []
    for j in range(N_SLABS):
        xs = x[:, j * LANES:(j + 1) * LANES]
        x2 = xs * xs
        sum_a = jnp.sum(jnp.where(head_a, x2, 0.0), axis=-1, keepdims=True)
        sum_b = jnp.sum(jnp.where(head_a, 0.0, x2), axis=-1, keepdims=True)
        inv = lax.rsqrt(jnp.where(head_a, sum_a, sum_b) * (1.0 / HEAD_DIM) + EPS)
        outs.append(xs * inv * gain[:, j * LANES:(j + 1) * LANES])
    return outs


def _mix_in_kernel(h_ref, g_ref, win_ref, cw_ref, qg_ref, kg_ref,
                   yc_ref, q_ref, k_ref, v_ref, carry_ref):
    @pl.when(pl.program_id(1) == 0)
    def _():
        carry_ref[...] = jnp.zeros_like(carry_ref)

    cwid = CONV_WIDTH
    cw = cw_ref[...]
    zs = []
    for u in range(h_ref.shape[0]):
        hn = _rms(h_ref[u], g_ref[...]).astype(BF16)
        zs.append(_dot(hn, win_ref[...]))
    for u, z in enumerate(zs):
        g_b, g_c, c_x = z[:, 0:cwid], z[:, cwid:2 * cwid], z[:, 2 * cwid:3 * cwid]
        x = g_c * c_x
        prev = carry_ref[u]
        conv = cw[0:1] * _shift_rows(x, prev, 2) + cw[1:2] * _shift_rows(x, prev, 1) + cw[2:3] * x
        yc_ref[u] = (g_b * conv).astype(BF16)
        carry_ref[u] = x[x.shape[0] - SUBLANES:, :]

        q = z[:, 3 * cwid:3 * cwid + ATTN_WIDTH]
        k = z[:, 3 * cwid + ATTN_WIDTH:3 * cwid + 2 * ATTN_WIDTH]
        v = z[:, 3 * cwid + 2 * ATTN_WIDTH:]
        qn = _head_rms(q, qg_ref[...] * (HEAD_DIM ** -0.5 * LOG2_E))
        kn = _head_rms(k, kg_ref[...])
        for j in range(N_SLABS):
            q_ref[u, j] = qn[j]
            k_ref[u, j] = kn[j]
            v_ref[u, j] = v[:, j * LANES:(j + 1) * LANES]


def _mix_in(h, norm, w_in, conv_w, q_gain, k_gain, *, batch, seq):
    t, d = h.shape
    tm = TOKEN_TILE
    n_seq = MIX_LANES
    tok = lambda b, j: (b, j, 0)
    slab = pl.BlockSpec((n_seq, N_SLABS, tm, LANES), lambda b, j: (b, 0, j, 0))
    qg = jnp.tile(q_gain, N_HEADS)[None, :]
    kg = jnp.tile(k_gain, N_HEADS)[None, :]
    slab_shape = jax.ShapeDtypeStruct((batch, N_SLABS, seq, LANES), F32)
    yc, q, k, v = pl.pallas_call(
        _mix_in_kernel,
        grid=(batch // n_seq, seq // tm),
        in_specs=[pl.BlockSpec((n_seq, tm, d), tok), _resident(norm.shape), _resident(w_in.shape),
                  _resident(conv_w.shape), _resident(qg.shape), _resident(kg.shape)],
        out_specs=[pl.BlockSpec((n_seq, tm, CONV_WIDTH), tok), slab, slab, slab],
        out_shape=[jax.ShapeDtypeStruct((batch, seq, CONV_WIDTH), BF16), slab_shape, slab_shape, slab_shape],
        scratch_shapes=[pltpu.VMEM((n_seq, SUBLANES, CONV_WIDTH), F32)],
        compiler_params=pltpu.CompilerParams(dimension_semantics=("parallel", "arbitrary"),
                                             vmem_limit_bytes=VMEM_LIMIT),
        name="mix_in",
    )(h.reshape(batch, seq, d), norm, w_in, conv_w, qg, kg)
    return yc.reshape(t, CONV_WIDTH), q, k, v


def _rel_bucket(dist):
    max_exact = REL_BUCKETS // 2
    n = jnp.maximum(dist, 1).astype(F32)
    large = max_exact + (jnp.log(n / max_exact) / math.log(REL_MAX_DIST / max_exact)
                         * (REL_BUCKETS - max_exact)).astype(jnp.int32)
    large = jnp.minimum(large, REL_BUCKETS - 1)
    return jnp.where(dist < max_exact, dist, large)


def _bias_table(rel_bias):
    qi = jnp.arange(KEYS_BACK)[:, None]
    kj = jnp.arange(2 * KEYS_BACK)[None, :]
    dist = qi + KEYS_BACK - kj
    valid = (dist >= 0) & (dist <= KEYS_BACK)
    tabs = []
    for d in DILATIONS:
        bucket = _rel_bucket(jnp.clip(dist, 0, KEYS_BACK) * d)
        onehot = (bucket[..., None] == jnp.arange(REL_BUCKETS)).astype(F32)
        bias = jnp.einsum("qkb,bh->hqk", onehot, rel_bias.astype(F32) * LOG2_E,
                          precision=lax.Precision.HIGHEST)
        tabs.append(jnp.stack([jnp.where(valid[None], bias, MASKED),
                               jnp.where((valid & (kj >= KEYS_BACK))[None], bias, MASKED)]))
    tab = jnp.stack(tabs)
    return tab.reshape(len(DILATIONS), 2, N_SLABS, 2 * KEYS_BACK, 2 * KEYS_BACK)


def _attn_kernel(q_ref, kp_ref, kc_ref, vp_ref, vc_ref, bias_ref, o_ref,
                 q4_scr, kp4_scr, kc4_scr, vp4_scr, vc4_scr, acc_scr, m_scr, l_scr):
    ch = ATTN_CHUNK
    nq = KEYS_BACK
    d1 = DILATIONS[1]
    quarter = ch // d1
    first_chunk = jnp.where(pl.program_id(2) == 0, 1, 0)
    head_a = lax.broadcasted_iota(jnp.int32, (1, LANES), 1) < HEAD_DIM
    ones = jnp.ones((2 * nq, LANES), BF16)

    for src, dst in ((q_ref, q4_scr), (kp_ref, kp4_scr), (kc_ref, kc4_scr), (vp_ref, vp4_scr), (vc_ref, vc4_scr)):
        for r in range(d1):
            dst[r * quarter:(r + 1) * quarter, :] = src[pl.ds(r, quarter, stride=d1), :]

    def attend(g, q, k, v, bias, out_rows):
        q2 = jnp.concatenate([jnp.where(head_a, q, 0.0), jnp.where(head_a, 0.0, q)], axis=0).astype(BF16)
        v_aug = jnp.concatenate([v.astype(BF16), ones], axis=-1)
        s = lax.dot_general(q2, k.astype(BF16), (((1,), (1,)), ((), ())), preferred_element_type=F32)
        s = s + bias
        m = jnp.max(s, axis=-1, keepdims=True)
        p = jnp.exp2(s - m).astype(BF16)
        pv = _dot(p, v_aug)
        acc_scr[g, out_rows, :] = jnp.where(head_a, pv[:nq, :LANES], pv[nq:, :LANES])
        l_scr[g, out_rows, :] = jnp.where(head_a, pv[:nq, LANES:], pv[nq:, LANES:])
        m_scr[g, out_rows, :] = jnp.where(head_a, m[:nq], m[nq:])

    def cat(lo, hi):
        return jnp.concatenate([lo, hi], axis=0)

    def unit(g, t):
        first = bias_ref[g, first_chunk]
        if g == 0:
            q_rows = pl.ds(t * nq, nq)
            if t == 0:
                lo, hi = pl.ds(ch - nq, nq), pl.ds(0, nq)
                attend(g, q_ref[q_rows, :], cat(kp_ref[lo, :], kc_ref[hi, :]), cat(vp_ref[lo, :], vc_ref[hi, :]),
                       first, q_rows)
            else:
                kv = pl.ds((t - 1) * nq, 2 * nq)
                attend(g, q_ref[q_rows, :], kc_ref[kv, :], vc_ref[kv, :], bias_ref[g, 0], q_rows)
        elif g == 1:
            r, i = t // d1, t % d1
            q_rows = pl.ds(r * quarter + i * nq, nq)
            if i == 0:
                lo, hi = pl.ds((r + 1) * quarter - nq, nq), pl.ds(r * quarter, nq)
                attend(g, q4_scr[q_rows, :], cat(kp4_scr[lo, :], kc4_scr[hi, :]),
                       cat(vp4_scr[lo, :], vc4_scr[hi, :]), first, q_rows)
            else:
                kv = pl.ds(r * quarter + (i - 1) * nq, 2 * nq)
                attend(g, q4_scr[q_rows, :], kc4_scr[kv, :], vc4_scr[kv, :], bias_ref[g, 0], q_rows)
        else:
            rows = pl.ds((t % d1) * quarter + t // d1, nq, stride=d1)
            attend(g, q4_scr[rows, :], cat(kp4_scr[rows, :], kc4_scr[rows, :]),
                   cat(vp4_scr[rows, :], vc4_scr[rows, :]), first, rows)

    def merge(r, blk):
        rows4 = pl.ds(r * quarter + blk * nq, nq)
        rows0 = pl.ds(r + blk * (d1 * nq), nq, stride=d1)
        sel = lambda scr: [scr[0, rows0, :], scr[1, rows4, :], scr[2, rows4, :]]
        ms = sel(m_scr)
        m_all = functools.reduce(jnp.maximum, ms)
        ws = [jnp.exp2(m - m_all) for m in ms]
        num = sum(w * a for w, a in zip(ws, sel(acc_scr)))
        den = sum(w * l for w, l in zip(ws, sel(l_scr)))
        o_ref[rows0, :] = num / den

    n_blocks = ch // nq
    per_res = n_blocks // d1
    for t in range(n_blocks):
        unit(0, t)
        if t % per_res == per_res - 1:
            unit(2, d1 * (t // per_res))
    for r in range(d1):
        for blk in range(per_res):
            unit(1, per_res * r + blk)
            merge(r, blk)
        if r + 1 < d1:
            for r2 in range(per_res):
                unit(2, r + 1 + d1 * r2)


def _attention(q, k, v, bias_tab):
    batch, _, seq, _ = q.shape
    ch = ATTN_CHUNK
    nch = seq // ch
    nb = len(DILATIONS)
    cur = pl.BlockSpec((None, None, ch, LANES), lambda b, s, c: (b, s, c, 0))
    prv = pl.BlockSpec((None, None, ch, LANES), lambda b, s, c: (b, s, jnp.maximum(c - 1, 0), 0))
    bias_spec = pl.BlockSpec((nb, 2, None, 2 * KEYS_BACK, 2 * KEYS_BACK), lambda b, s, c: (0, 0, s, 0, 0))
    return pl.pallas_call(
        _attn_kernel,
        grid=(batch, N_SLABS, nch),
        in_specs=[cur, prv, cur, prv, cur, bias_spec],
        out_specs=cur,
        out_shape=jax.ShapeDtypeStruct(q.shape, F32),
        scratch_shapes=[pltpu.VMEM((ch, LANES), F32)] * 5 + [pltpu.VMEM((nb, ch, LANES), F32)] * 3,
        compiler_params=pltpu.CompilerParams(dimension_semantics=("parallel", "parallel", "parallel"),
                                             vmem_limit_bytes=VMEM_LIMIT),
        name="dilated_attention",
    )(q, k, k, v, v, bias_tab)


def _rec_kernel(h_ref, g_ref, win_ref, cw_ref, cb_ref, wa_ref, ba_ref, wx_ref, bx_ref, lam_ref, wout_ref,
                o_ref, x_scr, a_scr, b_scr, hl_scr, pl_scr, carry_scr):
    n_seq, tm, width = h_ref.shape
    n_slabs = width // LANES
    seg = tm // SUBLANES + 1

    @pl.when(pl.program_id(1) == 0)
    def _():
        x_scr[:, 0:SUBLANES, :] = jnp.zeros((n_seq, SUBLANES, width), F32)
        carry_scr[...] = jnp.zeros_like(carry_scr)

    neg_lam = -lam_ref[...]
    softplus = jnp.maximum(neg_lam, 0.0) + jnp.log1p(jnp.exp(-jnp.abs(neg_lam)))
    half_c = (0.5 * LRU_C) * softplus
    cw = cw_ref[...]
    blk = width // LRU_BLOCKS
    gates = []
    for u in range(n_seq):
        hn = _rms(h_ref[u], g_ref[...]).astype(BF16)
        z = _dot(hn, win_ref[...])
        gates.append(z[:, width:])
        x_scr[u, SUBLANES:, :] = z[:, :width]
        xb = cb_ref[...] + cw[3:4] * x_scr[u, SUBLANES:, :]
        for k in range(1, cw.shape[0]):
            xb = xb + cw[3 - k:4 - k] * x_scr[u, pl.ds(SUBLANES - k, tm), :]
        x_scr[u, 0:SUBLANES, :] = x_scr[u, tm:, :]

        xb16 = xb.astype(BF16)
        half_ga = jnp.concatenate([_dot(xb16[:, n * blk:(n + 1) * blk], wa_ref[n]) for n in range(LRU_BLOCKS)],
                                  axis=-1) + ba_ref[...]
        half_gx = jnp.concatenate([_dot(xb16[:, n * blk:(n + 1) * blk], wx_ref[n]) for n in range(LRU_BLOCKS)],
                                  axis=-1) + bx_ref[...]
        neg_log_a = half_c * jnp.tanh(half_ga) + half_c
        a = jnp.exp2(neg_log_a * (-LOG2_E))
        one_minus_a2 = jnp.tanh(neg_log_a) * (1.0 + a * a)
        root = jnp.where(one_minus_a2 > 0.0, one_minus_a2 * lax.rsqrt(one_minus_a2), 0.0)
        b = root * ((0.5 * jnp.tanh(half_gx) + 0.5) * xb)
        for c in range(n_slabs):
            a_scr[u, c, 0:tm, :] = a[:, c * LANES:(c + 1) * LANES]
            b_scr[u, c, 0:tm, :] = b[:, c * LANES:(c + 1) * LANES]
            a_scr[u, c, tm:, :] = jnp.ones((SUBLANES * seg - tm, LANES), F32)
            b_scr[u, c, tm:, :] = jnp.zeros((SUBLANES * seg - tm, LANES), F32)

    def scan(u):
        def local_scan(j, carry):
            hl, pr = carry
            step = pl.ds(j, SUBLANES, stride=seg)
            out = pl.ds(pl.multiple_of(j * SUBLANES, SUBLANES), SUBLANES)
            hl_new, pr_new = [], []
            for c in range(n_slabs):
                a_j = a_scr[u, c, step, :]
                hl_c = a_j * hl[c] + b_scr[u, c, step, :]
                pr_c = a_j * pr[c]
                hl_scr[c, out, :] = hl_c
                pl_scr[c, out, :] = pr_c
                hl_new.append(hl_c)
                pr_new.append(pr_c)
            return tuple(hl_new), tuple(pr_new)

        zeros = tuple(jnp.zeros((SUBLANES, LANES), F32) for _ in range(n_slabs))
        ones = tuple(jnp.ones((SUBLANES, LANES), F32) for _ in range(n_slabs))
        hl_end, pr_end = lax.fori_loop(0, seg, local_scan, (zeros, ones), unroll=5)

        starts = []
        for c in range(n_slabs):
            state = carry_scr[u, c, 0:1, :]
            rows = []
            for s in range(SUBLANES):
                rows.append(state)
                state = pr_end[c][s:s + 1, :] * state + hl_end[c][s:s + 1, :]
            carry_scr[u, c, 0:1, :] = state
            starts.append(jnp.concatenate(rows, axis=0))

        def apply_carry(j, carry):
            step = pl.ds(j, SUBLANES, stride=seg)
            src = pl.ds(pl.multiple_of(j * SUBLANES, SUBLANES), SUBLANES)
            for c in range(n_slabs):
                a_scr[u, c, step, :] = hl_scr[c, src, :] + pl_scr[c, src, :] * starts[c]
            return carry

        lax.fori_loop(0, seg, apply_carry, 0, unroll=5)

    for u in range(n_seq):
        scan(u)

    for u in range(n_seq):
        hs = jnp.concatenate([a_scr[u, c, 0:tm, :] for c in range(n_slabs)], axis=-1)
        y = (hs * _gelu_tanh(gates[u])).astype(BF16)
        o_ref[u] = h_ref[u] + _dot(y, wout_ref[...])


def _recurrent(h, norm, w_in, conv_w, conv_b, wa, ba, wx, bx, lam, w_out, *, batch, seq):
    t, d = h.shape
    tm = TOKEN_TILE
    n_seq = REC_LANES
    tok = pl.BlockSpec((n_seq, tm, d), lambda b, j: (b, j, 0))
    consts = [norm, w_in, conv_w, conv_b, wa, ba, wx, bx, lam, w_out]
    n_slabs = d // LANES
    scan_rows = SUBLANES * (tm // SUBLANES + 1)
    out = pl.pallas_call(
        _rec_kernel,
        grid=(batch // n_seq, seq // tm),
        in_specs=[tok] + [_resident(c.shape) for c in consts],
        out_specs=tok,
        out_shape=jax.ShapeDtypeStruct((batch, seq, d), F32),
        scratch_shapes=[pltpu.VMEM((n_seq, tm + SUBLANES, d), F32),
                        pltpu.VMEM((n_seq, n_slabs, scan_rows, LANES), F32),
                        pltpu.VMEM((n_seq, n_slabs, scan_rows, LANES), F32),
                        pltpu.VMEM((n_slabs, scan_rows, LANES), F32), pltpu.VMEM((n_slabs, scan_rows, LANES), F32),
                        pltpu.VMEM((n_seq, n_slabs, SUBLANES, LANES), F32)],
        compiler_params=pltpu.CompilerParams(dimension_semantics=("parallel", "arbitrary"),
                                             vmem_limit_bytes=VMEM_LIMIT),
        name="recurrent",
    )(h.reshape(batch, seq, d), *consts)
    return out.reshape(t, d)


def kernel(x, p, rel_bias, ffn1_norm, ffn1_w_gate, ffn1_w_up, ffn1_w_down, mix_norm, hyb_w_in, hyb_conv_w, hyb_q_gain, hyb_k_gain, hyb_w_out, rec_w_in, rec_conv_w, rec_conv_b, lru_wa, lru_ba, lru_wx, lru_bx, lru_lambda, rec_w_out, ffn2_norm, ffn2_w_gate, ffn2_w_up, ffn2_w_down, ple_norm, ple_w_gate, ple_w_proj):
    batch, seq, d = x.shape
    depth = p.shape[0]
    t = batch * seq
    assert seq % ATTN_CHUNK == 0 and seq % (TOKEN_TILE * FFN_SUBTILES) == 0 and batch % REC_LANES == 0 \
        and batch % MIX_LANES == 0 and DILATIONS == (1, DILATIONS[1], DILATIONS[1] ** 2)
    row = lambda a: a[None, :].astype(F32)
    w16 = lambda a: a.astype(BF16)

    bias_tab = _bias_table(rel_bias)
    h = x.reshape(t, d)
    rows = lambda a: a[:, None, :]
    ple = (p.reshape(depth, t, -1), rows(ple_norm), ple_w_gate, ple_w_proj)
    for i in range(depth):
        h = _ffn(h, i, rows(ffn1_norm), ffn1_w_gate, ffn1_w_up, ffn1_w_down, seq=seq)
        ffn2 = (i, rows(ffn2_norm), ffn2_w_gate, ffn2_w_up, ffn2_w_down)
        if i % 2 == 0:
            e = i // 2
            yc, q, k, v = _mix_in(h, row(mix_norm[i]), w16(hyb_w_in[e]), hyb_conv_w[e],
                                  hyb_q_gain[e], hyb_k_gain[e], batch=batch, seq=seq)
            ya = _attention(q, k, v, bias_tab)
            h = _ffn(h, *ffn2, mix=(yc, ya, hyb_w_out, e), ple=ple, seq=seq)
        else:
            o = i // 2
            h = _recurrent(h, row(mix_norm[i]), w16(rec_w_in[o]), rec_conv_w[o], row(rec_conv_b[o]),
                           w16(0.5 * lru_wa[o]), row(0.5 * lru_ba[o]), w16(0.5 * lru_wx[o]), row(0.5 * lru_bx[o]),
                           row(lru_lambda[o]), w16(rec_w_out[o]), batch=batch, seq=seq)
            h = _ffn(h, *ffn2, ple=ple, seq=seq)
    return h.reshape(batch, seq, d)
```

```python
import functools
import math

import jax
import jax.numpy as jnp
from jax import lax
from jax.experimental import pallas as pl
from jax.experimental.pallas import tpu as pltpu

F32 = jnp.float32
BF16 = jnp.bfloat16

EPS = 1e-6
HEAD_DIM = 64
N_HEADS = 8
ATTN_WIDTH = N_HEADS * HEAD_DIM
CONV_WIDTH = 512
DILATIONS = (1, 4, 16)
KEYS_BACK = 128
REL_BUCKETS = 32
REL_MAX_DIST = 2048
LRU_BLOCKS = 4
LRU_C = 8.0
LOG2_E = math.log2(math.e)
LANES = 128
SUBLANES = 8
N_SLABS = ATTN_WIDTH // LANES
ATTN_CHUNK = KEYS_BACK * DILATIONS[-1]
MASKED = -1e30
VMEM_LIMIT = 56 * 1024 * 1024

TOKEN_TILE = 512
FFN_WARMUP = 8
FFN_SUBTILES = 2
REC_LANES = 2
MIX_LANES = 2


def _rms(x, gain):
    ms = jnp.mean(x * x, axis=-1, keepdims=True)
    return x * lax.rsqrt(ms + EPS) * gain


def _gelu_tanh(x):
    k0 = math.sqrt(2.0 / math.pi)
    inner = x * (k0 + (k0 * 0.044715) * (x * x))
    return (0.5 * x) * (1.0 + jnp.tanh(inner))


def _dot(a, b):
    return jnp.dot(a, b, preferred_element_type=F32)


def _resident(shape):
    nd = len(shape)
    return pl.BlockSpec(shape, lambda *_: (0,) * nd, pipeline_mode=pl.Buffered(1))


def _ffn_kernel(has_mix, has_ple, *refs):
    it = iter(refs)
    x_ref = next(it)
    if has_mix:
        yc_ref, ya_ref, wo_ref = next(it), next(it), next(it)
    g_ref, wg_ref, wu_ref, wd_ref = next(it), next(it), next(it), next(it)
    if has_ple:
        p_ref, pg_ref, pwg_ref, pwp_ref = next(it), next(it), next(it), next(it)
    o_ref = next(it)
    wg_s, wu_s, wd_s = next(it), next(it), next(it)
    casts = [(wg_ref, wg_s), (wu_ref, wu_s), (wd_ref, wd_s)]
    if has_mix:
        wo_s = next(it)
        casts.append((wo_ref, wo_s))
    if has_ple:
        pwg_s, pwp_s = next(it), next(it)
        casts += [(pwg_ref, pwg_s), (pwp_ref, pwp_s)]
    s = pl.program_id(0)

    @pl.when(s < FFN_WARMUP)
    def _():
        for src, dst in casts:
            rows = src.shape[0]
            dst[pl.ds(pl.multiple_of(s * rows, rows), rows), :] = src[...].astype(BF16)

    @pl.when(s >= FFN_WARMUP)
    def _():
        tm = TOKEN_TILE
        for sub in range(x_ref.shape[0] // tm):
            rows = slice(sub * tm, (sub + 1) * tm)
            x = x_ref[rows, :]
            if has_mix:
                ya = jnp.concatenate([ya_ref[j, rows, :] for j in range(N_SLABS)], axis=-1).astype(BF16)
                ymix = jnp.concatenate([yc_ref[rows, :], ya], axis=-1)
                x = x + _dot(ymix, wo_s[...])
            hn = _rms(x, g_ref[...]).astype(BF16)
            gate = _dot(hn, wg_s[...])
            up = _dot(hn, wu_s[...])
            act = (gate * jax.nn.sigmoid(gate) * up).astype(BF16)
            x = x + 0.5 * _dot(act, wd_s[...])
            if has_ple:
                hn2 = _rms(x, pg_ref[...]).astype(BF16)
                pgate = jax.nn.sigmoid(_dot(hn2, pwg_s[...]))
                x = x + pgate * _dot(p_ref[rows, :].astype(BF16), pwp_s[...])
            o_ref[rows, :] = x


def _ffn(h, layer, norm, w_gate, w_up, w_down, mix=None, ple=None, *, seq):
    t, d = h.shape
    tb = TOKEN_TILE * (1 if mix is not None else FFN_SUBTILES)
    seq_blocks = seq // tb
    warm = FFN_WARMUP
    blk = lambda s: jnp.maximum(s - warm, 0)
    tok = lambda s: (blk(s), 0)

    def weight(w, idx=layer):
        rows = w.shape[1] // warm
        return pl.BlockSpec((None, rows, w.shape[2]), lambda s: (idx, jnp.minimum(s, warm - 1), 0))

    def vec(g):
        return pl.BlockSpec((None, 1, g.shape[2]), lambda s: (layer, 0, 0))

    def resident(w):
        return pltpu.VMEM(w.shape[1:], BF16)

    args, specs = [h], [pl.BlockSpec((tb, d), tok)]
    scratch = [resident(w_gate), resident(w_up), resident(w_down)]
    if mix is not None:
        yc, ya, w_out, mix_layer = mix
        args += [yc, ya, w_out]
        specs += [pl.BlockSpec((tb, yc.shape[1]), tok),
                  pl.BlockSpec((None, N_SLABS, tb, LANES),
                               lambda s: (blk(s) // seq_blocks, 0, blk(s) % seq_blocks, 0)),
                  weight(w_out, mix_layer)]
        scratch.append(resident(w_out))
    args += [norm, w_gate, w_up, w_down]
    specs += [vec(norm), weight(w_gate), weight(w_up), weight(w_down)]
    if ple is not None:
        p, pnorm, pw_gate, pw_proj = ple
        args += [p, pnorm, pw_gate, pw_proj]
        specs += [pl.BlockSpec((None, tb, p.shape[2]), lambda s: (layer, blk(s), 0)), vec(pnorm),
                  weight(pw_gate), weight(pw_proj)]
        scratch += [resident(pw_gate), resident(pw_proj)]
    return pl.pallas_call(
        functools.partial(_ffn_kernel, mix is not None, ple is not None),
        grid=(warm + t // tb,),
        in_specs=specs,
        out_specs=pl.BlockSpec((tb, d), tok),
        out_shape=jax.ShapeDtypeStruct((t, d), F32),
        scratch_shapes=scratch,
        compiler_params=pltpu.CompilerParams(dimension_semantics=("arbitrary",), vmem_limit_bytes=VMEM_LIMIT),
        name="ffn" + ("_mix" if mix is not None else "") + ("_ple" if ple is not None else ""),
    )(*args)


def _shift_rows(u, prev, k):
    rolled = pltpu.roll(u, shift=k, axis=0)
    row = lax.broadcasted_iota(jnp.int32, u.shape, 0)
    for j in range(k):
        rolled = jnp.where(row == j, prev[SUBLANES - k + j:SUBLANES - k + j + 1, :], rolled)
    return rolled


def _head_rms(x, gain):
    head_a = lax.broadcasted_iota(jnp.int32, (1, LANES), 1) < HEAD_DIM
    outs = []
    for j in range(N_SLABS):
        xs = x[:, j * LANES:(j + 1) * LANES]
        x2 = xs * xs
        sum_a = jnp.sum(jnp.where(head_a, x2, 0.0), axis=-1, keepdims=True)
        sum_b = jnp.sum(jnp.where(head_a, 0.0, x2), axis=-1, keepdims=True)
        inv = lax.rsqrt(jnp.where(head_a, sum_a, sum_b) * (1.0 / HEAD_DIM) + EPS)
        outs.append(xs * inv * gain[:, j * LANES:(j + 1) * LANES])
    return outs


def _mix_in_kernel(h_ref, g_ref, win_ref, cw_ref, qg_ref, kg_ref,
                   yc_ref, q_ref, k_ref, v_ref, carry_ref):
    @pl.when(pl.program_id(1) == 0)
    def _():
        carry_ref[...] = jnp.zeros_like(carry_ref)

    cwid = CONV_WIDTH
    cw = cw_ref[...]
    zs = []
    for u in range(h_ref.shape[0]):
        hn = _rms(h_ref[u], g_ref[...]).astype(BF16)
        zs.append(_dot(hn, win_ref[...]))
    for u, z in enumerate(zs):
        g_b, g_c, c_x = z[:, 0:cwid], z[:, cwid:2 * cwid], z[:, 2 * cwid:3 * cwid]
        x = g_c * c_x
        prev = carry_ref[u]
        conv = cw[0:1] * _shift_rows(x, prev, 2) + cw[1:2] * _shift_rows(x, prev, 1) + cw[2:3] * x
        yc_ref[u] = (g_b * conv).astype(BF16)
        carry_ref[u] = x[x.shape[0] - SUBLANES:, :]

        q = z[:, 3 * cwid:3 * cwid + ATTN_WIDTH]
        k = z[:, 3 * cwid + ATTN_WIDTH:3 * cwid + 2 * ATTN_WIDTH]
        v = z[:, 3 * cwid + 2 * ATTN_WIDTH:]
        qn = _head_rms(q, qg_ref[...] * (HEAD_DIM ** -0.5 * LOG2_E))
        kn = _head_rms(k, kg_ref[...])
        for j in range(N_SLABS):
            q_ref[u, j] = qn[j]
            k_ref[u, j] = kn[j]
            v_ref[u, j] = v[:, j * LANES:(j + 1) * LANES]


def _mix_in(h, norm, w_in, conv_w, q_gain, k_gain, *, batch, seq):
    t, d = h.shape
    tm = TOKEN_TILE
    n_seq = MIX_LANES
    tok = lambda b, j: (b, j, 0)
    slab = pl.BlockSpec((n_seq, N_SLABS, tm, LANES), lambda b, j: (b, 0, j, 0))
    qg = jnp.tile(q_gain, N_HEADS)[None, :]
    kg = jnp.tile(k_gain, N_HEADS)[None, :]
    slab_shape = jax.ShapeDtypeStruct((batch, N_SLABS, seq, LANES), F32)
    yc, q, k, v = pl.pallas_call(
        _mix_in_kernel,
        grid=(batch // n_seq, seq // tm),
        in_specs=[pl.BlockSpec((n_seq, tm, d), tok), _resident(norm.shape), _resident(w_in.shape),
                  _resident(conv_w.shape), _resident(qg.shape), _resident(kg.shape)],
        out_specs=[pl.BlockSpec((n_seq, tm, CONV_WIDTH), tok), slab, slab, slab],
        out_shape=[jax.ShapeDtypeStruct((batch, seq, CONV_WIDTH), BF16), slab_shape, slab_shape, slab_shape],
        scratch_shapes=[pltpu.VMEM((n_seq, SUBLANES, CONV_WIDTH), F32)],
        compiler_params=pltpu.CompilerParams(dimension_semantics=("parallel", "arbitrary"),
                                             vmem_limit_bytes=VMEM_LIMIT),
        name="mix_in",
    )(h.reshape(batch, seq, d), norm, w_in, conv_w, qg, kg)
    return yc.reshape(t, CONV_WIDTH), q, k, v


def _rel_bucket(dist):
    max_exact = REL_BUCKETS // 2
    n = jnp.maximum(dist, 1).astype(F32)
    large = max_exact + (jnp.log(n / max_exact) / math.log(REL_MAX_DIST / max_exact)
                         * (REL_BUCKETS - max_exact)).astype(jnp.int32)
    large = jnp.minimum(large, REL_BUCKETS - 1)
    return jnp.where(dist < max_exact, dist, large)


def _bias_table(rel_bias):
    qi = jnp.arange(KEYS_BACK)[:, None]
    kj = jnp.arange(2 * KEYS_BACK)[None, :]
    dist = qi + KEYS_BACK - kj
    valid = (dist >= 0) & (dist <= KEYS_BACK)
    tabs = []
    for d in DILATIONS:
        bucket = _rel_bucket(jnp.clip(dist, 0, KEYS_BACK) * d)
        onehot = (bucket[..., None] == jnp.arange(REL_BUCKETS)).astype(F32)
        bias = jnp.einsum("qkb,bh->hqk", onehot, rel_bias.astype(F32) * LOG2_E,
                          precision=lax.Precision.HIGHEST)
        tabs.append(jnp.stack([jnp.where(valid[None], bias, MASKED),
                               jnp.where((valid & (kj >= KEYS_BACK))[None], bias, MASKED)]))
    tab = jnp.stack(tabs)
    return tab.reshape(len(DILATIONS), 2, N_SLABS, 2 * KEYS_BACK, 2 * KEYS_BACK)


def _attn_kernel(q_ref, kp_ref, kc_ref, vp_ref, vc_ref, bias_ref, o_ref,
                 q4_scr, kp4_scr, kc4_scr, vp4_scr, vc4_scr, acc_scr, m_scr, l_scr):
    ch = ATTN_CHUNK
    nq = KEYS_BACK
    d1 = DILATIONS[1]
    quarter = ch // d1
    first_chunk = jnp.where(pl.program_id(2) == 0, 1, 0)
    head_a = lax.broadcasted_iota(jnp.int32, (1, LANES), 1) < HEAD_DIM
    ones = jnp.ones((2 * nq, LANES), BF16)

    for src, dst in ((q_ref, q4_scr), (kp_ref, kp4_scr), (kc_ref, kc4_scr), (vp_ref, vp4_scr), (vc_ref, vc4_scr)):
        for r in range(d1):
            dst[r * quarter:(r + 1) * quarter, :] = src[pl.ds(r, quarter, stride=d1), :]

    def attend(g, q, k, v, bias, out_rows):
        q2 = jnp.concatenate([jnp.where(head_a, q, 0.0), jnp.where(head_a, 0.0, q)], axis=0).astype(BF16)
        v_aug = jnp.concatenate([v.astype(BF16), ones], axis=-1)
        s = lax.dot_general(q2, k.astype(BF16), (((1,), (1,)), ((), ())), preferred_element_type=F32)
        s = s + bias
        m = jnp.max(s, axis=-1, keepdims=True)
        p = jnp.exp2(s - m).astype(BF16)
        pv = _dot(p, v_aug)
        acc_scr[g, out_rows, :] = jnp.where(head_a, pv[:nq, :LANES], pv[nq:, :LANES])
        l_scr[g, out_rows, :] = jnp.where(head_a, pv[:nq, LANES:], pv[nq:, LANES:])
        m_scr[g, out_rows, :] = jnp.where(head_a, m[:nq], m[nq:])

    def cat(lo, hi):
        return jnp.concatenate([lo, hi], axis=0)

    def unit(g, t):
        first = bias_ref[g, first_chunk]
        if g == 0:
            q_rows = pl.ds(t * nq, nq)
            if t == 0:
                lo, hi = pl.ds(ch - nq, nq), pl.ds(0, nq)
                attend(g, q_ref[q_rows, :], cat(kp_ref[lo, :], kc_ref[hi, :]), cat(vp_ref[lo, :], vc_ref[hi, :]),
                       first, q_rows)
            else:
                kv = pl.ds((t - 1) * nq, 2 * nq)
                attend(g, q_ref[q_rows, :], kc_ref[kv, :], vc_ref[kv, :], bias_ref[g, 0], q_rows)
        elif g == 1:
            r, i = t // d1, t % d1
            q_rows = pl.ds(r * quarter + i * nq, nq)
            if i == 0:
                lo, hi = pl.ds((r + 1) * quarter - nq, nq), pl.ds(r * quarter, nq)
                attend(g, q4_scr[q_rows, :], cat(kp4_scr[lo, :], kc4_scr[hi, :]),
                       cat(vp4_scr[lo, :], vc4_scr[hi, :]), first, q_rows)
            else:
                kv = pl.ds(r * quarter + (i - 1) * nq, 2 * nq)
                attend(g, q4_scr[q_rows, :], kc4_scr[kv, :], vc4_scr[kv, :], bias_ref[g, 0], q_rows)
        else:
            rows = pl.ds((t % d1) * quarter + t // d1, nq, stride=d1)
            attend(g, q4_scr[rows, :], cat(kp4_scr[rows, :], kc4_scr[rows, :]),
                   cat(vp4_scr[rows, :], vc4_scr[rows, :]), first, rows)

    def merge(r, blk):
        rows4 = pl.ds(r * quarter + blk * nq, nq)
        rows0 = pl.ds(r + blk * (d1 * nq), nq, stride=d1)
        sel = lambda scr: [scr[0, rows0, :], scr[1, rows4, :], scr[2, rows4, :]]
        ms = sel(m_scr)
        m_all = functools.reduce(jnp.maximum, ms)
        ws = [jnp.exp2(m - m_all) for m in ms]
        num = sum(w * a for w, a in zip(ws, sel(acc_scr)))
        den = sum(w * l for w, l in zip(ws, sel(l_scr)))
        o_ref[rows0, :] = num / den

    n_blocks = ch // nq
    per_res = n_blocks // d1
    for t in range(n_blocks):
        unit(0, t)
        if t % per_res == per_res - 1:
            unit(2, d1 * (t // per_res))
    for r in range(d1):
        for blk in range(per_res):
            unit(1, per_res * r + blk)
            merge(r, blk)
        if r + 1 < d1:
            for r2 in range(per_res):
                unit(2, r + 1 + d1 * r2)


def _attention(q, k, v, bias_tab):
    batch, _, seq, _ = q.shape
    ch = ATTN_CHUNK
    nch = seq // ch
    nb = len(DILATIONS)
    cur = pl.BlockSpec((None, None, ch, LANES), lambda b, s, c: (b, s, c, 0))
    prv = pl.BlockSpec((None, None, ch, LANES), lambda b, s, c: (b, s, jnp.maximum(c - 1, 0), 0))
    bias_spec = pl.BlockSpec((nb, 2, None, 2 * KEYS_BACK, 2 * KEYS_BACK), lambda b, s, c: (0, 0, s, 0, 0))
    return pl.pallas_call(
        _attn_kernel,
        grid=(batch, N_SLABS, nch),
        in_specs=[cur, prv, cur, prv, cur, bias_spec],
        out_specs=cur,
        out_shape=jax.ShapeDtypeStruct(q.shape, F32),
        scratch_shapes=[pltpu.VMEM((ch, LANES), F32)] * 5 + [pltpu.VMEM((nb, ch, LANES), F32)] * 3,
        compiler_params=pltpu.CompilerParams(dimension_semantics=("parallel", "parallel", "parallel"),
                                             vmem_limit_bytes=VMEM_LIMIT),
        name="dilated_attention",
    )(q, k, k, v, v, bias_tab)


def _rec_kernel(h_ref, g_ref, win_ref, cw_ref, cb_ref, wa_ref, ba_ref, wx_ref, bx_ref, lam_ref, wout_ref,
                o_ref, x_scr, a_scr, b_scr, hl_scr, pl_scr, carry_scr):
    n_seq, tm, width = h_ref.shape
    n_slabs = width // LANES
    seg = tm // SUBLANES + 1

    @pl.when(pl.program_id(1) == 0)
    def _():
        x_scr[:, 0:SUBLANES, :] = jnp.zeros((n_seq, SUBLANES, width), F32)
        carry_scr[...] = jnp.zeros_like(carry_scr)

    neg_lam = -lam_ref[...]
    softplus = jnp.maximum(neg_lam, 0.0) + jnp.log1p(jnp.exp(-jnp.abs(neg_lam)))
    half_c = (0.5 * LRU_C) * softplus
    cw = cw_ref[...]
    blk = width // LRU_BLOCKS
    gates = []
    for u in range(n_seq):
        hn = _rms(h_ref[u], g_ref[...]).astype(BF16)
        z = _dot(hn, win_ref[...])
        gates.append(z[:, width:])
        x_scr[u, SUBLANES:, :] = z[:, :width]
    for u in range(n_seq):
        xb = cb_ref[...] + cw[3:4] * x_scr[u, SUBLANES:, :]
        for k in range(1, cw.shape[0]):
            xb = xb + cw[3 - k:4 - k] * x_scr[u, pl.ds(SUBLANES - k, tm), :]
        x_scr[u, 0:SUBLANES, :] = x_scr[u, tm:, :]

        xb16 = xb.astype(BF16)
        half_ga = jnp.concatenate([_dot(xb16[:, n * blk:(n + 1) * blk], wa_ref[n]) for n in range(LRU_BLOCKS)],
                                  axis=-1) + ba_ref[...]
        half_gx = jnp.concatenate([_dot(xb16[:, n * blk:(n + 1) * blk], wx_ref[n]) for n in range(LRU_BLOCKS)],
                                  axis=-1) + bx_ref[...]
        neg_log_a = half_c * jnp.tanh(half_ga) + half_c
        a = jnp.exp2(neg_log_a * (-LOG2_E))
        one_minus_a2 = jnp.tanh(neg_log_a) * (1.0 + a * a)
        root = jnp.where(one_minus_a2 > 0.0, one_minus_a2 * lax.rsqrt(one_minus_a2), 0.0)
        b = root * ((0.5 * jnp.tanh(half_gx) + 0.5) * xb)
        for c in range(n_slabs):
            a_scr[u, c, 0:tm, :] = a[:, c * LANES:(c + 1) * LANES]
            b_scr[u, c, 0:tm, :] = b[:, c * LANES:(c + 1) * LANES]
            a_scr[u, c, tm:, :] = jnp.ones((SUBLANES * seg - tm, LANES), F32)
            b_scr[u, c, tm:, :] = jnp.zeros((SUBLANES * seg - tm, LANES), F32)

    def scan(u):
        def local_scan(j, carry):
            hl, pr = carry
            step = pl.ds(j, SUBLANES, stride=seg)
            out = pl.ds(pl.multiple_of(j * SUBLANES, SUBLANES), SUBLANES)
            hl_new, pr_new = [], []
            for c in range(n_slabs):
                a_j = a_scr[u, c, step, :]
                hl_c = a_j * hl[c] + b_scr[u, c, step, :]
                pr_c = a_j * pr[c]
                hl_scr[c, out, :] = hl_c
                pl_scr[c, out, :] = pr_c
                hl_new.append(hl_c)
                pr_new.append(pr_c)
            return tuple(hl_new), tuple(pr_new)

        zeros = tuple(jnp.zeros((SUBLANES, LANES), F32) for _ in range(n_slabs))
        ones = tuple(jnp.ones((SUBLANES, LANES), F32) for _ in range(n_slabs))
        hl_end, pr_end = lax.fori_loop(0, seg, local_scan, (zeros, ones), unroll=5)

        starts = []
        for c in range(n_slabs):
            state = carry_scr[u, c, 0:1, :]
            rows = []
            for s in range(SUBLANES):
                rows.append(state)
                state = pr_end[c][s:s + 1, :] * state + hl_end[c][s:s + 1, :]
            carry_scr[u, c, 0:1, :] = state
            starts.append(jnp.concatenate(rows, axis=0))

        def apply_carry(j, carry):
            step = pl.ds(j, SUBLANES, stride=seg)
            src = pl.ds(pl.multiple_of(j * SUBLANES, SUBLANES), SUBLANES)
            for c in range(n_slabs):
                a_scr[u, c, step, :] = hl_scr[c, src, :] + pl_scr[c, src, :] * starts[c]
            return carry

        lax.fori_loop(0, seg, apply_carry, 0, unroll=5)

    for u in range(n_seq):
        scan(u)

    for u in range(n_seq):
        hs = jnp.concatenate([a_scr[u, c, 0:tm, :] for c in range(n_slabs)], axis=-1)
        y = (hs * _gelu_tanh(gates[u])).astype(BF16)
        o_ref[u] = h_ref[u] + _dot(y, wout_ref[...])


def _recurrent(h, norm, w_in, conv_w, conv_b, wa, ba, wx, bx, lam, w_out, *, batch, seq):
    t, d = h.shape
    tm = TOKEN_TILE
    n_seq = REC_LANES
    tok = pl.BlockSpec((n_seq, tm, d), lambda b, j: (b, j, 0))
    consts = [norm, w_in, conv_w, conv_b, wa, ba, wx, bx, lam, w_out]
    n_slabs = d // LANES
    scan_rows = SUBLANES * (tm // SUBLANES + 1)
    out = pl.pallas_call(
        _rec_kernel,
        grid=(batch // n_seq, seq // tm),
        in_specs=[tok] + [_resident(c.shape) for c in consts],
        out_specs=tok,
        out_shape=jax.ShapeDtypeStruct((batch, seq, d), F32),
        scratch_shapes=[pltpu.VMEM((n_seq, tm + SUBLANES, d), F32),
                        pltpu.VMEM((n_seq, n_slabs, scan_rows, LANES), F32),
                        pltpu.VMEM((n_seq, n_slabs, scan_rows, LANES), F32),
                        pltpu.VMEM((n_slabs, scan_rows, LANES), F32), pltpu.VMEM((n_slabs, scan_rows, LANES), F32),
                        pltpu.VMEM((n_seq, n_slabs, SUBLANES, LANES), F32)],
        compiler_params=pltpu.CompilerParams(dimension_semantics=("parallel", "arbitrary"),
                                             vmem_limit_bytes=VMEM_LIMIT),
        name="recurrent",
    )(h.reshape(batch, seq, d), *consts)
    return out.reshape(t, d)


def kernel(x, p, rel_bias, ffn1_norm, ffn1_w_gate, ffn1_w_up, ffn1_w_down, mix_norm, hyb_w_in, hyb_conv_w, hyb_q_gain, hyb_k_gain, hyb_w_out, rec_w_in, rec_conv_w, rec_conv_b, lru_wa, lru_ba, lru_wx, lru_bx, lru_lambda, rec_w_out, ffn2_norm, ffn2_w_gate, ffn2_w_up, ffn2_w_down, ple_norm, ple_w_gate, ple_w_proj):
    batch, seq, d = x.shape
    depth = p.shape[0]
    t = batch * seq
    assert seq % ATTN_CHUNK == 0 and seq % (TOKEN_TILE * FFN_SUBTILES) == 0 and batch % REC_LANES == 0 \
        and batch % MIX_LANES == 0 and DILATIONS == (1, DILATIONS[1], DILATIONS[1] ** 2)
    row = lambda a: a[None, :].astype(F32)
    w16 = lambda a: a.astype(BF16)

    bias_tab = _bias_table(rel_bias)
    h = x.reshape(t, d)
    rows = lambda a: a[:, None, :]
    ple = (p.reshape(depth, t, -1), rows(ple_norm), ple_w_gate, ple_w_proj)
    for i in range(depth):
        h = _ffn(h, i, rows(ffn1_norm), ffn1_w_gate, ffn1_w_up, ffn1_w_down, seq=seq)
        ffn2 = (i, rows(ffn2_norm), ffn2_w_gate, ffn2_w_up, ffn2_w_down)
        if i % 2 == 0:
            e = i // 2
            yc, q, k, v = _mix_in(h, row(mix_norm[i]), w16(hyb_w_in[e]), hyb_conv_w[e],
                                  hyb_q_gain[e], hyb_k_gain[e], batch=batch, seq=seq)
            ya = _attention(q, k, v, bias_tab)
            h = _ffn(h, *ffn2, mix=(yc, ya, hyb_w_out, e), ple=ple, seq=seq)
        else:
            o = i // 2
            h = _recurrent(h, row(mix_norm[i]), w16(rec_w_in[o]), rec_conv_w[o], row(rec_conv_b[o]),
                           w16(0.5 * lru_wa[o]), row(0.5 * lru_ba[o]), w16(0.5 * lru_wx[o]), row(0.5 * lru_bx[o]),
                           row(lru_lambda[o]), w16(rec_w_out[o]), batch=batch, seq=seq)
            h = _ffn(h, *ffn2, ple=ple, seq=seq)
    return h.reshape(batch, seq, d)
```
